```python
import math
import jax
import jax.numpy as jnp
from jax import lax
import numpy as np

D_MODEL = 1024
BATCH = 2
SEQ = 8192
DEPTH = 4
DEC_BATCH = 128
DEC_SEQ = 1
PAST_LEN = 8192
PAGE_SIZE = 128

N_HEADS = 8
N_KV_HEADS = 2
HEAD_DIM = 64
Q_PER_KV = N_HEADS // N_KV_HEADS
D_ATTN = N_HEADS * HEAD_DIM
D_KV = N_KV_HEADS * HEAD_DIM
WINDOW = 128
ATTN_BLOCK = 128
ROPE_THETA = 10000.0
W_BUF = min(WINDOW, PAST_LEN)
SSD_HEADS = 4
SSD_HEAD_DIM = 64
D_SSD = SSD_HEADS * SSD_HEAD_DIM
SSD_GROUPS = 2
SSD_STATE = 64
SSD_CONV = 4
SSD_CHUNK = 128
D_XBC = D_SSD + 2 * SSD_GROUPS * SSD_STATE
D_CONV = 256
CONV_WIDTH = 3
D_MIX = D_ATTN + D_SSD + D_CONV
SPLIT_SIZES = (D_ATTN, D_KV, D_KV, D_SSD, D_XBC, SSD_HEADS, D_CONV, D_CONV, D_CONV)
D_IN = sum(SPLIT_SIZES)
D_FF = 2816
N_EXPERTS = 8
TOP_K = 2
N_DENSE = (DEPTH + 1) // 2
N_MOE = DEPTH // 2
ALPHA = (2 * DEPTH) ** 0.25
BETA = (8 * DEPTH) ** -0.25
NORM_EPS = 1e-5

kernel_name = 'hymba_swa_ssd_shortconv_deepnorm_moe_step'


def _split_points():
    pts, acc = [], 0
    for s in SPLIT_SIZES[:-1]:
        acc += s
        pts.append(acc)
    return pts


def _layernorm(x, g, b):
    xf = x.astype(jnp.float32)
    mu = jnp.mean(xf, -1, keepdims=True)
    var = jnp.mean(jnp.square(xf - mu), -1, keepdims=True)
    return ((xf - mu) * lax.rsqrt(var + NORM_EPS) * g.astype(jnp.float32) + b.astype(jnp.float32)).astype(x.dtype)


def _rope(x, pos):
    half = HEAD_DIM // 2
    inv_freq = ROPE_THETA ** (-jnp.arange(half, dtype=jnp.float32) / half)
    ang = pos.astype(jnp.float32)[:, None] * inv_freq[None, :]
    cos = jnp.cos(ang)[:, None, :]
    sin = jnp.sin(ang)[:, None, :]
    xf = x.astype(jnp.float32)
    x1, x2 = xf[..., :half], xf[..., half:]
    return jnp.concatenate([x1 * cos - x2 * sin, x2 * cos + x1 * sin], -1).astype(x.dtype)


def _causal_dwconv(u, buf, w):
    k_w = w.shape[0]
    t = u.shape[1]
    full = jnp.concatenate([buf.astype(u.dtype), u], axis=1)
    out = full[:, 0:t] * w[0]
    for j in range(1, k_w):
        out = out + full[:, j:j + t] * w[j]
    return out, full[:, t:]


def _sink_attention(q, k, v, q_pos, k_pos, sinks):
    s = jnp.einsum('...qkgd,...skd->...kgqs', q.astype(jnp.float32), k.astype(jnp.float32)) / math.sqrt(HEAD_DIM)
    qp = q_pos[..., :, None]
    kp = k_pos[..., None, :]
    allowed = (kp <= qp) & (kp >= qp - WINDOW) & (kp >= 0)
    s = jnp.where(allowed[..., None, None, :, :], s, -jnp.inf)
    sink = sinks.astype(jnp.float32).reshape(N_KV_HEADS, Q_PER_KV)[:, :, None, None]
    m = jnp.maximum(jnp.max(s, -1, keepdims=True), sink)
    p = jnp.exp(s - m)
    denom = jnp.sum(p, -1, keepdims=True) + jnp.exp(sink - m)
    return jnp.einsum('...kgqs,...skd->...qkgd', (p / denom).astype(v.dtype), v)


def _attn_prompt(q, k, v, sinks):
    b, t = q.shape[:2]
    nb = t // ATTN_BLOCK
    qb = q.reshape(b, nb, ATTN_BLOCK, N_KV_HEADS, Q_PER_KV, HEAD_DIM)
    kb = k.reshape(b, nb, ATTN_BLOCK, N_KV_HEADS, HEAD_DIM)
    vb = v.reshape(b, nb, ATTN_BLOCK, N_KV_HEADS, HEAD_DIM)
    kprev = jnp.concatenate([jnp.zeros_like(kb[:, :1]), kb[:, :-1]], axis=1)
    vprev = jnp.concatenate([jnp.zeros_like(vb[:, :1]), vb[:, :-1]], axis=1)
    kk = jnp.concatenate([kprev, kb], axis=2)
    vv = jnp.concatenate([vprev, vb], axis=2)
    pos = jnp.arange(t, dtype=jnp.int32).reshape(nb, ATTN_BLOCK)
    k_pos = jnp.concatenate([pos - ATTN_BLOCK, pos], axis=1)
    o = _sink_attention(qb, kk, vv, pos, k_pos, sinks)
    return o.reshape(b, t, D_ATTN)


def _ssd_scan(x, dt, a, bm, cm, h0):
    b, t = x.shape[:2]
    L = SSD_CHUNK if t % SSD_CHUNK == 0 else t
    nc = t // L
    rep = SSD_HEADS // SSD_GROUPS
    bh = jnp.repeat(bm, rep, axis=2).reshape(b, nc, L, SSD_HEADS, SSD_STATE)
    ch = jnp.repeat(cm, rep, axis=2).reshape(b, nc, L, SSD_HEADS, SSD_STATE)
    xdt = (x * dt[..., None]).reshape(b, nc, L, SSD_HEADS, SSD_HEAD_DIM)
    cum = jnp.cumsum((dt * a).reshape(b, nc, L, SSD_HEADS), axis=2)
    causal = jnp.tril(jnp.ones((L, L), dtype=bool))[None, None, :, :, None]
    diff = cum[:, :, :, None, :] - cum[:, :, None, :, :]
    decay = jnp.exp(jnp.where(causal, diff, -jnp.inf))
    scores = jnp.einsum('bclhn,bcshn->bclsh', ch, bh) * decay
    y = jnp.einsum('bclsh,bcshp->bclhp', scores, xdt)
    w_end = jnp.exp(cum[:, :, -1:, :] - cum)
    states = jnp.einsum('bclhn,bclh,bclhp->bchpn', bh, w_end, xdt)
    chunk_decay = jnp.exp(cum[:, :, -1, :])

    def step(h, inp):
        s_c, d_c = inp
        return h * d_c[:, :, None, None] + s_c, h

    h_last, h_prev = lax.scan(step, h0, (jnp.swapaxes(states, 0, 1), jnp.swapaxes(chunk_decay, 0, 1)))
    h_prev = jnp.swapaxes(h_prev, 0, 1)
    y = y + jnp.einsum('bclhn,bchpn,bclh->bclhp', ch, h_prev, jnp.exp(cum))
    return y.reshape(b, t, SSD_HEADS, SSD_HEAD_DIM), h_last


def _mixer(h, pos, state, w_in, sinks, ssd_cw, ssd_cb, dt_bias, a_log, d_skip, norm_w, sc_w, w_out):
    b, t, _ = h.shape
    q, k, v, z, xbc, dt_raw, gate_b, gate_c, u = jnp.split(h @ w_in, _split_points(), axis=-1)
    q = _rope(q.reshape(b, t, N_HEADS, HEAD_DIM), pos)
    k = _rope(k.reshape(b, t, N_KV_HEADS, HEAD_DIM), pos)
    v = v.reshape(b, t, N_KV_HEADS, HEAD_DIM)
    if state is None:
        o_attn = _attn_prompt(q, k, v, sinks)
        keep = min(WINDOW, t)
        k_buf, v_buf = k[:, t - keep:], v[:, t - keep:]
        ssd_buf = jnp.zeros((b, SSD_CONV - 1, D_XBC), h.dtype)
        h0 = jnp.zeros((b, SSD_HEADS, SSD_HEAD_DIM, SSD_STATE), jnp.float32)
        sc_buf = jnp.zeros((b, CONV_WIDTH - 1, D_CONV), h.dtype)
    else:
        c_k, c_v, h0, ssd_buf, sc_buf = state
        n_buf = c_k.shape[1]
        k_all = jnp.concatenate([c_k.astype(k.dtype), k], axis=1)
        v_all = jnp.concatenate([c_v.astype(v.dtype), v], axis=1)
        k_pos = pos[0] - n_buf + jnp.arange(n_buf + t, dtype=jnp.int32)
        qg = q.reshape(b, t, N_KV_HEADS, Q_PER_KV, HEAD_DIM)
        o_attn = _sink_attention(qg, k_all, v_all, pos, k_pos, sinks).reshape(b, t, D_ATTN)
        k_buf, v_buf = k_all[:, t:], v_all[:, t:]
        h0 = h0.astype(jnp.float32)
    xbc_c, ssd_buf_new = _causal_dwconv(xbc, ssd_buf, ssd_cw)
    xbc_c = jax.nn.silu(xbc_c + ssd_cb)
    xs, bm, cm = jnp.split(xbc_c, [D_SSD, D_SSD + SSD_GROUPS * SSD_STATE], axis=-1)
    xs = xs.reshape(b, t, SSD_HEADS, SSD_HEAD_DIM).astype(jnp.float32)
    bm = bm.reshape(b, t, SSD_GROUPS, SSD_STATE).astype(jnp.float32)
    cm = cm.reshape(b, t, SSD_GROUPS, SSD_STATE).astype(jnp.float32)
    dt = jax.nn.softplus(dt_raw.astype(jnp.float32) + dt_bias.astype(jnp.float32))
    a = -jnp.exp(a_log.astype(jnp.float32))
    y, h_last = _ssd_scan(xs, dt, a, bm, cm, h0)
    y = y + d_skip.astype(jnp.float32)[:, None] * xs
    y = y.reshape(b, t, D_SSD) * jax.nn.silu(z.astype(jnp.float32))
    y = y * lax.rsqrt(jnp.mean(jnp.square(y), -1, keepdims=True) + NORM_EPS) * norm_w.astype(jnp.float32)
    o_ssd = y.astype(h.dtype)
    cc, sc_buf_new = _causal_dwconv(gate_c * u, sc_buf, sc_w)
    o_conv = gate_b * cc
    out = jnp.concatenate([o_attn, o_ssd, o_conv], axis=-1) @ w_out
    return out, (k_buf, v_buf, h_last.astype(h.dtype), ssd_buf_new, sc_buf_new)


def _swiglu(x, wg, wu, wd):
    return (jax.nn.silu(x @ wg) * (x @ wu)) @ wd


def _moe(x, router, wg, wu, wd):
    logits = (x @ router).astype(jnp.float32)
    top_v, top_i = lax.top_k(logits, TOP_K)
    gates = jax.nn.softmax(top_v, axis=-1)
    comb = jnp.sum(jax.nn.one_hot(top_i, N_EXPERTS, dtype=jnp.float32) * gates[..., None], axis=-2)
    out = jnp.zeros_like(x)
    for e in range(N_EXPERTS):
        out = out + comb[..., e:e + 1].astype(x.dtype) * _swiglu(x, wg[e], wu[e], wd[e])
    return out


def _trunk(x, pos, states, weights):
    (w_in, attn_sinks, ssd_conv_w, ssd_conv_b, ssd_dt_bias, ssd_a_log, ssd_d, ssd_norm_w, sconv_w, w_out,
     ln1_g, ln1_b, ln2_g, ln2_b, ffn_w_gate, ffn_w_up, ffn_w_down,
     moe_router, moe_w_gate, moe_w_up, moe_w_down) = weights
    new = ([], [], [], [], [])
    for i in range(DEPTH):
        st = None if states is None else (states[0][i], states[1][i], states[2][i], states[3][i], states[4][i])
        m, ns = _mixer(x, pos, st, w_in[i], attn_sinks[i], ssd_conv_w[i], ssd_conv_b[i], ssd_dt_bias[i],
                       ssd_a_log[i], ssd_d[i], ssd_norm_w[i], sconv_w[i], w_out[i])
        for lst, s in zip(new, ns):
            lst.append(s)
        x = _layernorm(ALPHA * x + m, ln1_g[i], ln1_b[i])
        j = i // 2
        if i % 2 == 0:
            f = _swiglu(x, ffn_w_gate[j], ffn_w_up[j], ffn_w_down[j])
        else:
            f = _moe(x, moe_router[j], moe_w_gate[j], moe_w_up[j], moe_w_down[j])
        x = _layernorm(ALPHA * x + f, ln2_g[i], ln2_b[i])
    return x, [jnp.stack(l) for l in new]


def setup_inputs(seed: int = 0) -> dict:
    key = jax.random.key(seed)
    ks = jax.random.split(key, 32)
    f32 = jnp.float32
    nrm = lambda k, shape, scale: jax.random.normal(k, shape, f32) * scale
    u_dt = jax.random.uniform(ks[11], (DEPTH, SSD_HEADS), f32)
    dt0 = jnp.exp(u_dt * (math.log(0.1) - math.log(0.001)) + math.log(0.001))
    return {
        'x_prompt': nrm(ks[0], (BATCH, SEQ, D_MODEL), 1.0),
        'x_sample': nrm(ks[1], (DEC_BATCH, DEC_SEQ, D_MODEL), 1.0),
        'cache_win_k': nrm(ks[2], (DEPTH, DEC_BATCH, W_BUF, N_KV_HEADS, HEAD_DIM), 1.0),
        'cache_win_v': nrm(ks[3], (DEPTH, DEC_BATCH, W_BUF, N_KV_HEADS, HEAD_DIM), 1.0),
        'state_ssd': nrm(ks[4], (DEPTH, DEC_BATCH, SSD_HEADS, SSD_HEAD_DIM, SSD_STATE), 0.5),
        'state_ssd_conv': nrm(ks[5], (DEPTH, DEC_BATCH, SSD_CONV - 1, D_XBC), 1.0),
        'state_conv': nrm(ks[6], (DEPTH, DEC_BATCH, CONV_WIDTH - 1, D_CONV), 1.0),
        'w_in': nrm(ks[7], (DEPTH, D_MODEL, D_IN), D_MODEL ** -0.5),
        'attn_sinks': nrm(ks[8], (DEPTH, N_HEADS), 0.5),
        'ssd_conv_w': nrm(ks[9], (DEPTH, SSD_CONV, D_XBC), SSD_CONV ** -0.5),
        'ssd_conv_b': nrm(ks[10], (DEPTH, D_XBC), 0.02),
        'ssd_dt_bias': dt0 + jnp.log(-jnp.expm1(-dt0)),
        'ssd_a_log': jnp.log(jax.random.uniform(ks[12], (DEPTH, SSD_HEADS), f32, 1.0, 16.0)),
        'ssd_d': 1.0 + nrm(ks[13], (DEPTH, SSD_HEADS), 0.01),
        'ssd_norm_w': 1.0 + nrm(ks[14], (DEPTH, D_SSD), 0.01),
        'sconv_w': nrm(ks[15], (DEPTH, CONV_WIDTH, D_CONV), CONV_WIDTH ** -0.5),
        'w_out': nrm(ks[16], (DEPTH, D_MIX, D_MODEL), BETA * D_MIX ** -0.5),
        'ln1_g': 1.0 + nrm(ks[17], (DEPTH, D_MODEL), 0.01),
        'ln1_b': nrm(ks[18], (DEPTH, D_MODEL), 0.01),
        'ln2_g': 1.0 + nrm(ks[19], (DEPTH, D_MODEL), 0.01),
        'ln2_b': nrm(ks[20], (DEPTH, D_MODEL), 0.01),
        'ffn_w_gate': nrm(ks[21], (N_DENSE, D_MODEL, D_FF), D_MODEL ** -0.5),
        'ffn_w_up': nrm(ks[22], (N_DENSE, D_MODEL, D_FF), D_MODEL ** -0.5),
        'ffn_w_down': nrm(ks[23], (N_DENSE, D_FF, D_MODEL), BETA * D_FF ** -0.5),
        'moe_router': nrm(ks[24], (N_MOE, D_MODEL, N_EXPERTS), D_MODEL ** -0.5),
        'moe_w_gate': nrm(ks[25], (N_MOE, N_EXPERTS, D_MODEL, D_FF), D_MODEL ** -0.5),
        'moe_w_up': nrm(ks[26], (N_MOE, N_EXPERTS, D_MODEL, D_FF), D_MODEL ** -0.5),
        'moe_w_down': nrm(ks[27], (N_MOE, N_EXPERTS, D_FF, D_MODEL), BETA * D_FF ** -0.5),
    }


def reference(x_prompt, x_sample, cache_win_k, cache_win_v, state_ssd, state_ssd_conv, state_conv,
              w_in, attn_sinks, ssd_conv_w, ssd_conv_b, ssd_dt_bias, ssd_a_log, ssd_d, ssd_norm_w, sconv_w,
              w_out, ln1_g, ln1_b, ln2_g, ln2_b, ffn_w_gate, ffn_w_up, ffn_w_down,
              moe_router, moe_w_gate, moe_w_up, moe_w_down):
    weights = (w_in, attn_sinks, ssd_conv_w, ssd_conv_b, ssd_dt_bias, ssd_a_log, ssd_d, ssd_norm_w, sconv_w,
               w_out, ln1_g, ln1_b, ln2_g, ln2_b, ffn_w_gate, ffn_w_up, ffn_w_down,
               moe_router, moe_w_gate, moe_w_up, moe_w_down)
    pos_p = jnp.arange(x_prompt.shape[1], dtype=jnp.int32)
    y_prompt, st_p = _trunk(x_prompt, pos_p, None, weights)
    pos_s = PAST_LEN + jnp.arange(x_sample.shape[1], dtype=jnp.int32)
    y_sample, st_s = _trunk(x_sample, pos_s, (cache_win_k, cache_win_v, state_ssd, state_ssd_conv, state_conv), weights)
    p_k, p_v, p_ssd, p_ssd_conv, p_conv = st_p
    s_k, s_v, s_ssd, s_ssd_conv, s_conv = st_s
    return (y_prompt, y_sample, p_k, p_v, p_ssd, p_ssd_conv, p_conv, s_k, s_v, s_ssd, s_ssd_conv, s_conv)
```

```python
import functools
import math

import jax
import jax.numpy as jnp
from jax import lax
from jax.experimental import pallas as pl
from jax.experimental.pallas import tpu as pltpu

D_MODEL = 1024
BATCH = 2
SEQ = 8192
DEPTH = 4
DEC_BATCH = 128
PAST_LEN = 8192
N_HEADS = 8
N_KV_HEADS = 2
HEAD_DIM = 64
Q_PER_KV = N_HEADS // N_KV_HEADS
D_ATTN = N_HEADS * HEAD_DIM
D_KV = N_KV_HEADS * HEAD_DIM
WINDOW = 128
BLK = 128
ROPE_THETA = 10000.0
SSD_HEADS = 4
SSD_HEAD_DIM = 64
D_SSD = SSD_HEADS * SSD_HEAD_DIM
SSD_GROUPS = 2
SSD_STATE = 64
SSD_CONV = 4
D_XBC = D_SSD + 2 * SSD_GROUPS * SSD_STATE
D_CONV = 256
CONV_WIDTH = 3
D_MIX = D_ATTN + D_SSD + D_CONV
D_FF = 2816
N_EXPERTS = 8
ALPHA = (2 * DEPTH) ** 0.25
NORM_EPS = 1e-5

LANES = 128
N_PROMPT = BATCH * SEQ
N_TOK = N_PROMPT + DEC_BATCH
N_BLK = SEQ // BLK
TM = 512
N_TILES = pl.cdiv(N_TOK, TM)
VMEM_LIMIT = 56 * 1024 * 1024

C_Q, C_K, C_V, C_Z, C_XBC = 0, 512, 640, 768, 1024
C_GB, C_GC, C_U, C_DT = 1536, 1792, 2048, 2304
D_PROJ = C_DT + LANES

BF16 = jnp.bfloat16
F32 = jnp.float32


def _sigmoid(x):
    return 1.0 / (1.0 + jnp.exp(-x))


def _silu(x):
    return x * _sigmoid(x)


def _softplus(x):
    return jnp.maximum(x, 0.0) + jnp.log1p(jnp.exp(-jnp.abs(x)))


def _layernorm(xf, g, b):
    mu = jnp.mean(xf, -1, keepdims=True)
    xc = xf - mu
    var = jnp.mean(xc * xc, -1, keepdims=True)
    return xc * lax.rsqrt(var + NORM_EPS) * g + b


def _rope(x, cos, sin):
    w = x.shape[1]
    reps = w // LANES
    if reps > 1:
        cos = jnp.concatenate([cos] * reps, axis=1)
        sin = jnp.concatenate([sin] * reps, axis=1)
    lane = lax.broadcasted_iota(jnp.int32, x.shape, 1)
    first_half = (lane % HEAD_DIM) < (HEAD_DIM // 2)
    partner = jnp.where(first_half, pltpu.roll(x, w - HEAD_DIM // 2, 1), pltpu.roll(x, HEAD_DIM // 2, 1))
    return x * cos + partner * sin


def _dot(a, b):
    return jnp.dot(a.astype(BF16), b.astype(BF16), preferred_element_type=F32)


def _dot_nt(a, b):
    return lax.dot_general(a.astype(BF16), b.astype(BF16), (((1,), (1,)), ((), ())), preferred_element_type=F32)


def _dot_tn(a, b):
    return lax.dot_general(a.astype(BF16), b.astype(BF16), (((0,), (0,)), ((), ())), preferred_element_type=F32)


def _inproj_kernel(x_ref, w_ref, o_ref):
    o_ref[...] = jnp.dot(x_ref[...].astype(BF16), w_ref[...], preferred_element_type=F32)


def _inproj(x, w):
    return pl.pallas_call(
        _inproj_kernel,
        grid=(N_TILES,),
        in_specs=[pl.BlockSpec((TM, D_MODEL), lambda i: (i, 0)),
                  pl.BlockSpec((D_MODEL, D_PROJ), lambda i: (0, 0))],
        out_specs=pl.BlockSpec((TM, D_PROJ), lambda i: (i, 0)),
        out_shape=jax.ShapeDtypeStruct((N_TOK, D_PROJ), F32),
        compiler_params=pltpu.CompilerParams(dimension_semantics=("arbitrary",), vmem_limit_bytes=VMEM_LIMIT),
        name="inproj",
    )(x, w)


def _ssd_gate_norm(y, z, norm_w):
    y = y * _silu(z)
    return y * lax.rsqrt(jnp.mean(y * y, -1, keepdims=True) + NORM_EPS) * norm_w


def _mixer_prompt_kernel(sinks_ref, q_ref, k_ref, v_ref, z_ref, xbc_ref, gb_ref, gc_ref, u_ref, dt_ref,
                         cos_ref, sin_ref, cw_ref, cb_ref, dtb_ref, alog_ref, dskip_ref, nw_ref, scw_ref,
                         mix_ref, klast_ref, hlast_ref, culast_ref,
                         kprev, vprev, conv_ext, cu_ext, hstate):
    j = pl.program_id(1)

    @pl.when(j == 0)
    def _():
        kprev[...] = jnp.zeros_like(kprev)
        vprev[...] = jnp.zeros_like(vprev)
        conv_ext[0:8, :] = jnp.zeros((8, D_XBC), F32)
        cu_ext[0:8, :] = jnp.zeros((8, D_CONV), F32)
        hstate[...] = jnp.zeros_like(hstate)

    cos = cos_ref[...]
    sin = sin_ref[...]
    q = _rope(q_ref[...], cos, sin) * (1.0 / math.sqrt(HEAD_DIM))
    k = _rope(k_ref[...], cos, sin)
    v = v_ref[...]
    kp = kprev[...]
    vp = vprev[...]
    rows = Q_PER_KV * BLK
    r = lax.broadcasted_iota(jnp.int32, (rows, 2 * BLK), 0) % BLK
    c = lax.broadcasted_iota(jnp.int32, (rows, 2 * BLK), 1)
    allowed = (c >= r) & (c - BLK <= r) & ((c >= BLK) | (j > 0))
    heads = []
    for g in range(N_KV_HEADS):
        sl = slice(g * HEAD_DIM, (g + 1) * HEAD_DIM)
        k_all = jnp.concatenate([kp[:, sl], k[:, sl]], axis=0)
        v_all = jnp.concatenate([vp[:, sl], v[:, sl]], axis=0)
        qs = jnp.concatenate([q[:, (g * Q_PER_KV + h) * HEAD_DIM:(g * Q_PER_KV + h + 1) * HEAD_DIM]
                              for h in range(Q_PER_KV)], axis=0)
        s = jnp.where(allowed, _dot_nt(qs, k_all), -jnp.inf)
        sink = jnp.concatenate([jnp.full((BLK, 1), sinks_ref[g * Q_PER_KV + h], F32) for h in range(Q_PER_KV)],
                               axis=0)
        m = jnp.maximum(jnp.max(s, -1, keepdims=True), sink)
        p = jnp.exp(s - m)
        denom = jnp.sum(p, -1, keepdims=True) + jnp.exp(sink - m)
        o = _dot(p, v_all) * (1.0 / denom)
        heads += [o[h * BLK:(h + 1) * BLK, :] for h in range(Q_PER_KV)]
    mix_ref[:, 0:D_ATTN] = jnp.concatenate(heads, axis=1).astype(mix_ref.dtype)
    kprev[...] = k
    vprev[...] = v
    klast_ref[0] = k

    conv_ext[8:8 + BLK, :] = xbc_ref[...]
    acc = conv_ext[5:5 + BLK, :] * cw_ref[0:1, :]
    for t in range(1, SSD_CONV):
        acc = acc + conv_ext[5 + t:5 + t + BLK, :] * cw_ref[t:t + 1, :]
    conv_ext[0:8, :] = conv_ext[BLK:BLK + 8, :]
    xbc = _silu(acc + cb_ref[...])
    xs = xbc[:, 0:D_SSD]
    bm = xbc[:, D_SSD:D_SSD + LANES]
    cm = xbc[:, D_SSD + LANES:D_SSD + 2 * LANES]
    dt = _softplus(dt_ref[...] + dtb_ref[...])
    dta = dt * (-jnp.exp(alog_ref[...]))
    ti = lax.broadcasted_iota(jnp.int32, (BLK, BLK), 0)
    si = lax.broadcasted_iota(jnp.int32, (BLK, BLK), 1)
    causal = si <= ti
    cum = jnp.dot(causal.astype(F32), dta, preferred_element_type=F32, precision=lax.Precision.HIGHEST)
    cum_t = cum.T
    ys = []
    for g in range(SSD_GROUPS):
        sl = slice(g * SSD_STATE, (g + 1) * SSD_STATE)
        b_g = bm[:, sl]
        c_g = cm[:, sl]
        cb_scores = _dot_nt(c_g, b_g)
        for hh in range(SSD_HEADS // SSD_GROUPS):
            h = g * (SSD_HEADS // SSD_GROUPS) + hh
            ccol = cum[:, h:h + 1]
            crow = cum_t[h:h + 1, :]
            clast = cum[BLK - 1:BLK, h:h + 1]
            decay = jnp.exp(jnp.where(causal, ccol - crow, -jnp.inf))
            x_h = xs[:, h * SSD_HEAD_DIM:(h + 1) * SSD_HEAD_DIM]
            xdt = x_h * dt[:, h:h + 1]
            h_prev = hstate[h]
            y = _dot(cb_scores * decay, xdt)
            y = y + _dot_nt(c_g, h_prev) * jnp.exp(ccol)
            states = _dot_tn(xdt * jnp.exp(clast - ccol), b_g)
            hstate[h] = h_prev * jnp.exp(clast) + states
            ys.append(y)
    y = jnp.concatenate(ys, axis=1) + dskip_ref[...] * xs
    mix_ref[:, D_ATTN:D_ATTN + D_SSD] = _ssd_gate_norm(y, z_ref[...], nw_ref[...]).astype(mix_ref.dtype)
    hlast_ref[0] = hstate[...]

    cu_ext[8:8 + BLK, :] = gc_ref[...] * u_ref[...]
    cc = cu_ext[6:6 + BLK, :] * scw_ref[0:1, :]
    for t in range(1, CONV_WIDTH):
        cc = cc + cu_ext[6 + t:6 + t + BLK, :] * scw_ref[t:t + 1, :]
    tail = cu_ext[BLK:BLK + 8, :]
    cu_ext[0:8, :] = tail
    culast_ref[0] = tail
    mix_ref[:, D_ATTN + D_SSD:D_MIX] = (gb_ref[...] * cc).astype(mix_ref.dtype)


def _mixer_prompt(proj, sinks, cos_tab, sin_tab, cw, cb, dtb, alog, dskip, nw, scw):
    def pblk(width, col):
        return pl.BlockSpec((BLK, width), lambda b, j, *_: (b * N_BLK + j, col // width))

    def full(a):
        return pl.BlockSpec(a.shape, lambda b, j, *_: (0,) * a.ndim)

    params = (cw, cb, dtb, alog, dskip, nw, scw)
    grid_spec = pltpu.PrefetchScalarGridSpec(
        num_scalar_prefetch=1,
        grid=(BATCH, N_BLK),
        in_specs=[pblk(D_ATTN, C_Q), pblk(D_KV, C_K), pblk(D_KV, C_V), pblk(D_SSD, C_Z), pblk(D_XBC, C_XBC),
                  pblk(D_CONV, C_GB), pblk(D_CONV, C_GC), pblk(D_CONV, C_U), pblk(LANES, C_DT),
                  pl.BlockSpec((BLK, LANES), lambda b, j, *_: (j, 0)),
                  pl.BlockSpec((BLK, LANES), lambda b, j, *_: (j, 0))] + [full(a) for a in params],
        out_specs=[pl.BlockSpec((BLK, D_MIX), lambda b, j, *_: (b * N_BLK + j, 0)),
                   pl.BlockSpec((1, BLK, D_KV), lambda b, j, *_: (b, 0, 0)),
                   pl.BlockSpec((1, SSD_HEADS, SSD_HEAD_DIM, SSD_STATE), lambda b, j, *_: (b, 0, 0, 0)),
                   pl.BlockSpec((1, 8, D_CONV), lambda b, j, *_: (b, 0, 0))],
        scratch_shapes=[pltpu.VMEM((BLK, D_KV), F32), pltpu.VMEM((BLK, D_KV), F32),
                        pltpu.VMEM((BLK + 8, D_XBC), F32), pltpu.VMEM((BLK + 8, D_CONV), F32),
                        pltpu.VMEM((SSD_HEADS, SSD_HEAD_DIM, SSD_STATE), F32)],
    )
    return pl.pallas_call(
        _mixer_prompt_kernel,
        grid_spec=grid_spec,
        out_shape=[jax.ShapeDtypeStruct((N_TOK, D_MIX), BF16),
                   jax.ShapeDtypeStruct((BATCH, BLK, D_KV), F32),
                   jax.ShapeDtypeStruct((BATCH, SSD_HEADS, SSD_HEAD_DIM, SSD_STATE), F32),
                   jax.ShapeDtypeStruct((BATCH, 8, D_CONV), F32)],
        compiler_params=pltpu.CompilerParams(dimension_semantics=("arbitrary", "arbitrary"),
                                             vmem_limit_bytes=VMEM_LIMIT),
        name="mixer_prompt",
    )(sinks, *([proj] * 9), cos_tab, sin_tab, *params)


DEC_BB = 32


def _dec_attn_kernel(q_ref, k_ref, v_ref, kc_ref, vc_ref, cos_ref, sin_ref, sink_ref,
                     o_ref, knew_ref, qbd, kn8, vn8, o_scr):
    cos = cos_ref[...]
    sin = sin_ref[...]
    q = _rope(q_ref[...], cos, sin) * (1.0 / math.sqrt(HEAD_DIM))
    kn = _rope(k_ref[...], cos, sin)
    vn = v_ref[...]
    knew_ref[...] = kn
    lane = lax.broadcasted_iota(jnp.int32, (DEC_BB, LANES), 1)
    for h in range(N_HEADS):
        g = h // Q_PER_KV
        pair = q[:, (h // 2) * LANES:(h // 2 + 1) * LANES]
        if h % 2 != g:
            pair = pltpu.roll(pair, HEAD_DIM, 1)
        qbd[pl.ds(h, DEC_BB, stride=8), :] = jnp.where(lane // HEAD_DIM == g, pair, 0.0)
        kn8[pl.ds(h, DEC_BB, stride=8), :] = kn
        vn8[pl.ds(h, DEC_BB, stride=8), :] = vn
    q3 = qbd[...].reshape(DEC_BB, N_HEADS, LANES)
    kn3 = kn8[...].reshape(DEC_BB, N_HEADS, LANES)
    vn3 = vn8[...].reshape(DEC_BB, N_HEADS, LANES)
    s3 = jnp.einsum('bhl,bsl->bhs', q3.astype(BF16), kc_ref[...].astype(BF16), preferred_element_type=F32)
    s_self = jnp.sum(q3 * kn3, axis=-1, keepdims=True)
    sink3 = sink_ref[...][None, :, 0:1]
    m = jnp.maximum(jnp.maximum(jnp.max(s3, -1, keepdims=True), s_self), sink3)
    p3 = jnp.exp(s3 - m)
    p_self = jnp.exp(s_self - m)
    denom = jnp.sum(p3, -1, keepdims=True) + p_self + jnp.exp(sink3 - m)
    o3 = jnp.einsum('bhs,bsl->bhl', p3.astype(BF16), vc_ref[...].astype(BF16), preferred_element_type=F32)
    o3 = (o3 + p_self * vn3) * (1.0 / denom)
    h3 = lax.broadcasted_iota(jnp.int32, o3.shape, 1)
    l3 = lax.broadcasted_iota(jnp.int32, o3.shape, 2)
    o3 = jnp.where(l3 // HEAD_DIM == h3 // Q_PER_KV, o3, 0.0)
    o_scr[...] = o3.reshape(DEC_BB * N_HEADS, LANES)
    for pair in range(N_HEADS // 2):
        g = (2 * pair) // Q_PER_KV
        a = o_scr[pl.ds(2 * pair, DEC_BB, stride=8), :]
        b = o_scr[pl.ds(2 * pair + 1, DEC_BB, stride=8), :]
        if g == 1:
            a = pltpu.roll(a, HEAD_DIM, 1)
        else:
            b = pltpu.roll(b, HEAD_DIM, 1)
        o_ref[:, pair * LANES:(pair + 1) * LANES] = a + b


def _dec_attn(proj, kc, vc, cos1, sin1, sink_rows):
    base = N_PROMPT // DEC_BB

    def pblk(width, col):
        return pl.BlockSpec((DEC_BB, width), lambda i: (base + i, col // width))

    def full(a):
        return pl.BlockSpec(a.shape, lambda i: (0,) * a.ndim)

    return pl.pallas_call(
        _dec_attn_kernel,
        grid=(DEC_BATCH // DEC_BB,),
        in_specs=[pblk(D_ATTN, C_Q), pblk(D_KV, C_K), pblk(D_KV, C_V),
                  pl.BlockSpec((DEC_BB, WINDOW, D_KV), lambda i: (i, 0, 0)),
                  pl.BlockSpec((DEC_BB, WINDOW, D_KV), lambda i: (i, 0, 0)),
                  full(cos1), full(sin1), full(sink_rows)],
        out_specs=[pl.BlockSpec((DEC_BB, D_ATTN), lambda i: (i, 0)),
                   pl.BlockSpec((DEC_BB, D_KV), lambda i: (i, 0))],
        out_shape=[jax.ShapeDtypeStruct((DEC_BATCH, D_ATTN), F32),
                   jax.ShapeDtypeStruct((DEC_BATCH, D_KV), F32)],
        scratch_shapes=[pltpu.VMEM((DEC_BB * N_HEADS, LANES), F32)] * 4,
        compiler_params=pltpu.CompilerParams(dimension_semantics=("arbitrary",), vmem_limit_bytes=VMEM_LIMIT),
        name="dec_attn",
    )(proj, proj, proj, kc, vc, cos1, sin1, sink_rows)


D_STATE_FLAT = SSD_HEAD_DIM * SSD_STATE


def _dec_ssd_kernel(oattn_ref, z_ref, xbc_ref, gb_ref, gc_ref, u_ref, dt_ref, sbuf_ref, cbuf_ref, st_ref,
                    cw_ref, cb_ref, dtb_ref, alog_ref, dskip_ref, nw_ref, scw_ref, mixin_ref,
                    mix_ref, stnew_ref, sbufnew_ref, cbufnew_ref,
                    st_t, stnew_t, xdt_t, b_t, c_t, dec_t, y_t):
    del mixin_ref
    xbc_new = xbc_ref[...]
    acc = sbuf_ref[:, 0:D_XBC] * cw_ref[0:1, :]
    for t in range(1, SSD_CONV - 1):
        acc = acc + sbuf_ref[:, t * D_XBC:(t + 1) * D_XBC] * cw_ref[t:t + 1, :]
    acc = acc + xbc_new * cw_ref[SSD_CONV - 1:SSD_CONV, :]
    sbufnew_ref[:, 0:(SSD_CONV - 2) * D_XBC] = sbuf_ref[:, D_XBC:(SSD_CONV - 1) * D_XBC]
    sbufnew_ref[:, (SSD_CONV - 2) * D_XBC:(SSD_CONV - 1) * D_XBC] = xbc_new
    xbc = _silu(acc + cb_ref[...])
    xs = xbc[:, 0:D_SSD]
    dt = _softplus(dt_ref[...] + dtb_ref[...])
    dta = dt * (-jnp.exp(alog_ref[...]))
    dec_t[...] = jnp.exp(dta).T
    dt_t = dt.T
    xs_t = xs.T
    for h in range(SSD_HEADS):
        sl = slice(h * SSD_HEAD_DIM, (h + 1) * SSD_HEAD_DIM)
        xdt_t[sl, :] = xs_t[sl, :] * dt_t[h:h + 1, :]
    b_t[...] = xbc[:, D_SSD:D_SSD + LANES].T
    c_t[...] = xbc[:, D_SSD + LANES:D_SSD + 2 * LANES].T
    for h in range(SSD_HEADS):
        g = h // (SSD_HEADS // SSD_GROUPS)
        st_t[...] = st_ref[:, h * D_STATE_FLAT:(h + 1) * D_STATE_FLAT].T
        dec = dec_t[h:h + 1, :]
        bg = b_t[g * SSD_STATE:(g + 1) * SSD_STATE, :]
        cg = c_t[g * SSD_STATE:(g + 1) * SSD_STATE, :]

        def body(p, carry, h=h, dec=dec, bg=bg, cg=cg):
            off = pl.multiple_of(p * SSD_STATE, SSD_STATE)
            new = st_t[pl.ds(off, SSD_STATE), :] * dec + xdt_t[pl.ds(h * SSD_HEAD_DIM + p, 1), :] * bg
            stnew_t[pl.ds(off, SSD_STATE), :] = new
            y_t[pl.ds(h * SSD_HEAD_DIM + p, 1), :] = jnp.sum(new * cg, axis=0, keepdims=True)
            return carry

        lax.fori_loop(0, SSD_HEAD_DIM, body, 0)
        stnew_ref[:, h * D_STATE_FLAT:(h + 1) * D_STATE_FLAT] = stnew_t[...].T
    y = y_t[...].T + dskip_ref[...] * xs
    mix_ref[:, 0:D_ATTN] = oattn_ref[...].astype(mix_ref.dtype)
    mix_ref[:, D_ATTN:D_ATTN + D_SSD] = _ssd_gate_norm(y, z_ref[...], nw_ref[...]).astype(mix_ref.dtype)

    cu = gc_ref[...] * u_ref[...]
    cc = cbuf_ref[:, 0:D_CONV] * scw_ref[0:1, :] + cbuf_ref[:, D_CONV:2 * D_CONV] * scw_ref[1:2, :]
    cc = cc + cu * scw_ref[2:3, :]
    cbufnew_ref[:, 0:D_CONV] = cbuf_ref[:, D_CONV:2 * D_CONV]
    cbufnew_ref[:, D_CONV:2 * D_CONV] = cu
    mix_ref[:, D_ATTN + D_SSD:D_MIX] = (gb_ref[...] * cc).astype(mix_ref.dtype)


def _dec_ssd(mix, oattn, proj, sbuf, cbuf, st, cw, cb, dtb, alog, dskip, nw, scw):
    base = N_PROMPT // DEC_BATCH

    def pblk(width, col):
        return pl.BlockSpec((DEC_BATCH, width), lambda i: (base, col // width))

    def full(a):
        return pl.BlockSpec(a.shape, lambda i: (0,) * a.ndim)

    params = (cw, cb, dtb, alog, dskip, nw, scw)
    n_in = 10 + len(params)
    return pl.pallas_call(
        _dec_ssd_kernel,
        grid=(1,),
        in_specs=[full(oattn), pblk(D_SSD, C_Z), pblk(D_XBC, C_XBC), pblk(D_CONV, C_GB), pblk(D_CONV, C_GC),
                  pblk(D_CONV, C_U), pblk(LANES, C_DT), full(sbuf), full(cbuf), full(st)]
                 + [full(a) for a in params] + [pl.BlockSpec(memory_space=pl.ANY)],
        out_specs=[pl.BlockSpec((DEC_BATCH, D_MIX), lambda i: (base, 0)), full(st), full(sbuf), full(cbuf)],
        out_shape=[jax.ShapeDtypeStruct(mix.shape, mix.dtype), jax.ShapeDtypeStruct(st.shape, F32),
                   jax.ShapeDtypeStruct(sbuf.shape, F32), jax.ShapeDtypeStruct(cbuf.shape, F32)],
        scratch_shapes=[pltpu.VMEM((D_STATE_FLAT, DEC_BATCH), F32), pltpu.VMEM((D_STATE_FLAT, DEC_BATCH), F32),
                        pltpu.VMEM((D_SSD, DEC_BATCH), F32), pltpu.VMEM((LANES, DEC_BATCH), F32),
                        pltpu.VMEM((LANES, DEC_BATCH), F32), pltpu.VMEM((LANES, DEC_BATCH), F32),
                        pltpu.VMEM((D_SSD, DEC_BATCH), F32)],
        input_output_aliases={n_in: 0},
        compiler_params=pltpu.CompilerParams(dimension_semantics=("arbitrary",), vmem_limit_bytes=VMEM_LIMIT),
        name="dec_ssd",
    )(oattn, proj, proj, proj, proj, proj, proj, sbuf, cbuf, st, *params, mix)


def _outproj_kernel(mix_ref, x_ref, w_ref, g_ref, b_ref, o_ref):
    m = jnp.dot(mix_ref[...], w_ref[...], preferred_element_type=F32)
    o_ref[...] = _layernorm(ALPHA * x_ref[...] + m, g_ref[...], b_ref[...])


def _outproj_ln(mix, x, w, g, b):
    row = lambda i: (i, 0)
    const = lambda i: (0, 0)
    return pl.pallas_call(
        _outproj_kernel,
        grid=(N_TILES,),
        in_specs=[pl.BlockSpec((TM, D_MIX), row), pl.BlockSpec((TM, D_MODEL), row),
                  pl.BlockSpec((D_MIX, D_MODEL), const), pl.BlockSpec((1, D_MODEL), const),
                  pl.BlockSpec((1, D_MODEL), const)],
        out_specs=pl.BlockSpec((TM, D_MODEL), row),
        out_shape=jax.ShapeDtypeStruct((N_TOK, D_MODEL), F32),
        compiler_params=pltpu.CompilerParams(dimension_semantics=("arbitrary",), vmem_limit_bytes=VMEM_LIMIT),
        name="outproj_ln",
    )(mix, x, w, g, b)


def _swiglu(xb, wg, wu, wd):
    gate = jnp.dot(xb, wg, preferred_element_type=F32)
    up = jnp.dot(xb, wu, preferred_element_type=F32)
    return jnp.dot((_silu(gate) * up).astype(BF16), wd, preferred_element_type=F32)


def _ffn_kernel(x_ref, wg_ref, wu_ref, wd_ref, g_ref, b_ref, o_ref):
    x = x_ref[...]
    f = _swiglu(x.astype(BF16), wg_ref[...], wu_ref[...], wd_ref[...])
    o_ref[...] = _layernorm(ALPHA * x + f, g_ref[...], b_ref[...])


def _ffn_ln(x, wg, wu, wd, g, b):
    row = lambda i: (i, 0)
    const = lambda i: (0, 0)
    whole = pl.BlockSpec(memory_space=pltpu.VMEM)
    return pl.pallas_call(
        _ffn_kernel,
        grid=(N_TILES,),
        in_specs=[pl.BlockSpec((TM, D_MODEL), row), whole, whole, whole,
                  pl.BlockSpec((1, D_MODEL), const), pl.BlockSpec((1, D_MODEL), const)],
        out_specs=pl.BlockSpec((TM, D_MODEL), row),
        out_shape=jax.ShapeDtypeStruct((N_TOK, D_MODEL), F32),
        compiler_params=pltpu.CompilerParams(dimension_semantics=("arbitrary",), vmem_limit_bytes=VMEM_LIMIT),
        name="ffn_ln",
    )(x, wg, wu, wd, g, b)


def _moe_kernel(x_ref, r_ref, wg_ref, wu_ref, wd_ref, g_ref, b_ref, o_ref, comb, acc):
    e = pl.program_id(1)
    x = x_ref[...]
    lane = lax.broadcasted_iota(jnp.int32, (TM, LANES), 1)

    @pl.when(e == 0)
    def _():
        logits = jnp.dot(x, r_ref[...], preferred_element_type=F32, precision=lax.Precision.HIGHEST)
        logits = jnp.where(lane < N_EXPERTS, logits, -jnp.inf)
        v1 = jnp.max(logits, -1, keepdims=True)
        i1 = jnp.min(jnp.where(logits == v1, lane, LANES), -1, keepdims=True)
        rest = jnp.where(lane == i1, -jnp.inf, logits)
        v2 = jnp.max(rest, -1, keepdims=True)
        i2 = jnp.min(jnp.where(rest == v2, lane, LANES), -1, keepdims=True)
        e2 = jnp.exp(v2 - v1)
        g1 = 1.0 / (1.0 + e2)
        comb[...] = jnp.where(lane == i1, g1, 0.0) + jnp.where(lane == i2, e2 * g1, 0.0)
        acc[...] = jnp.zeros_like(acc)

    w_e = jnp.sum(jnp.where(lane == e, comb[...], 0.0), -1, keepdims=True)
    acc[...] += w_e * _swiglu(x.astype(BF16), wg_ref[0], wu_ref[0], wd_ref[0])

    @pl.when(e == N_EXPERTS - 1)
    def _():
        o_ref[...] = _layernorm(ALPHA * x + acc[...], g_ref[...], b_ref[...])


def _moe_ln(x, router, wg, wu, wd, g, b):
    row = lambda i, e: (i, 0)
    const = lambda i, e: (0, 0)
    expert = lambda i, e: (e, 0, 0)
    return pl.pallas_call(
        _moe_kernel,
        grid=(N_TILES, N_EXPERTS),
        in_specs=[pl.BlockSpec((TM, D_MODEL), row), pl.BlockSpec((D_MODEL, LANES), const),
                  pl.BlockSpec((1, D_MODEL, D_FF), expert), pl.BlockSpec((1, D_MODEL, D_FF), expert),
                  pl.BlockSpec((1, D_FF, D_MODEL), expert),
                  pl.BlockSpec((1, D_MODEL), const), pl.BlockSpec((1, D_MODEL), const)],
        out_specs=pl.BlockSpec((TM, D_MODEL), row),
        out_shape=jax.ShapeDtypeStruct((N_TOK, D_MODEL), F32),
        scratch_shapes=[pltpu.VMEM((TM, LANES), F32), pltpu.VMEM((TM, D_MODEL), F32)],
        compiler_params=pltpu.CompilerParams(dimension_semantics=("arbitrary", "arbitrary"),
                                             vmem_limit_bytes=VMEM_LIMIT),
        name="moe_ln",
    )(x, router, wg, wu, wd, g, b)


def _rope_tables(pos):
    half = HEAD_DIM // 2
    inv_freq = ROPE_THETA ** (-jnp.arange(half, dtype=F32) / half)
    ang = pos.astype(F32)[:, None] * inv_freq[None, :]
    cos = jnp.cos(ang)
    sin = jnp.sin(ang)
    return jnp.concatenate([cos] * 4, axis=1), jnp.concatenate([-sin, sin, -sin, sin], axis=1)


def _pad_lanes(a):
    return jnp.pad(a, [(0, 0)] * (a.ndim - 1) + [(0, LANES - a.shape[-1])])


def kernel(x_prompt, x_sample, cache_win_k, cache_win_v, state_ssd, state_ssd_conv, state_conv, w_in, attn_sinks,
           ssd_conv_w, ssd_conv_b, ssd_dt_bias, ssd_a_log, ssd_d, ssd_norm_w, sconv_w, w_out, ln1_g, ln1_b, ln2_g,
           ln2_b, ffn_w_gate, ffn_w_up, ffn_w_down, moe_router, moe_w_gate, moe_w_up, moe_w_down):
    x = jnp.concatenate([x_prompt.reshape(N_PROMPT, D_MODEL), x_sample.reshape(DEC_BATCH, D_MODEL)], axis=0)
    cos_p, sin_p = _rope_tables(jnp.arange(SEQ, dtype=jnp.int32))
    cos_s, sin_s = _rope_tables(PAST_LEN + jnp.arange(1, dtype=jnp.int32))
    split = C_GB + SSD_HEADS
    w_in_r = jnp.concatenate([w_in[:, :, :C_GB], w_in[:, :, split:], _pad_lanes(w_in[:, :, C_GB:split])],
                             axis=-1).astype(BF16)
    w_out_b = w_out.astype(BF16)
    dtb = _pad_lanes(ssd_dt_bias)[:, None, :]
    alog = _pad_lanes(ssd_a_log)[:, None, :]
    dskip = jnp.repeat(ssd_d, SSD_HEAD_DIM, axis=-1)[:, None, :]
    router = _pad_lanes(moe_router)
    sink_rows = jnp.broadcast_to(attn_sinks[:, :, None], (DEPTH, N_HEADS, LANES))
    kc = cache_win_k.reshape(DEPTH, DEC_BATCH, WINDOW, D_KV)
    vc = cache_win_v.reshape(DEPTH, DEC_BATCH, WINDOW, D_KV)
    st = state_ssd.reshape(DEPTH, DEC_BATCH, SSD_HEADS * D_STATE_FLAT)
    sbuf = state_ssd_conv.reshape(DEPTH, DEC_BATCH, (SSD_CONV - 1) * D_XBC)
    cbuf = state_conv.reshape(DEPTH, DEC_BATCH, (CONV_WIDTH - 1) * D_CONV)

    outs = {n: [] for n in ("pk", "pv", "ph", "psc", "pc", "sk", "sv", "sh", "ssc", "sc")}
    for i in range(DEPTH):
        proj = _inproj(x, w_in_r[i])
        params = (ssd_conv_w[i], ssd_conv_b[i][None, :], dtb[i], alog[i], dskip[i], ssd_norm_w[i][None, :],
                  sconv_w[i])
        mix, klast, hlast, culast = _mixer_prompt(proj, attn_sinks[i], cos_p, sin_p, *params)
        oattn, knew = _dec_attn(proj, kc[i], vc[i], cos_s, sin_s, sink_rows[i])
        mix, stnew, sbufnew, cbufnew = _dec_ssd(mix, oattn, proj, sbuf[i], cbuf[i], st[i], *params)
        x = _outproj_ln(mix, x, w_out_b[i], ln1_g[i][None, :], ln1_b[i][None, :])
        j = i // 2
        if i % 2 == 0:
            x = _ffn_ln(x, ffn_w_gate[j].astype(BF16), ffn_w_up[j].astype(BF16), ffn_w_down[j].astype(BF16),
                        ln2_g[i][None, :], ln2_b[i][None, :])
        else:
            x = _moe_ln(x, router[j], moe_w_gate[j].astype(BF16), moe_w_up[j].astype(BF16),
                        moe_w_down[j].astype(BF16), ln2_g[i][None, :], ln2_b[i][None, :])
        pp = proj[:N_PROMPT].reshape(BATCH, SEQ, D_PROJ)
        outs["pk"].append(klast.reshape(BATCH, WINDOW, N_KV_HEADS, HEAD_DIM))
        outs["pv"].append(pp[:, SEQ - WINDOW:, C_V:C_V + D_KV].reshape(BATCH, WINDOW, N_KV_HEADS, HEAD_DIM))
        outs["ph"].append(hlast)
        outs["psc"].append(pp[:, SEQ - (SSD_CONV - 1):, C_XBC:C_XBC + D_XBC])
        outs["pc"].append(culast[:, 8 - (CONV_WIDTH - 1):, :])
        vnew = proj[N_PROMPT:, C_V:C_V + D_KV]
        outs["sk"].append(jnp.concatenate([kc[i][:, 1:], knew[:, None, :]], axis=1)
                          .reshape(DEC_BATCH, WINDOW, N_KV_HEADS, HEAD_DIM))
        outs["sv"].append(jnp.concatenate([vc[i][:, 1:], vnew[:, None, :]], axis=1)
                          .reshape(DEC_BATCH, WINDOW, N_KV_HEADS, HEAD_DIM))
        outs["sh"].append(stnew.reshape(DEC_BATCH, SSD_HEADS, SSD_HEAD_DIM, SSD_STATE))
        outs["ssc"].append(sbufnew.reshape(DEC_BATCH, SSD_CONV - 1, D_XBC))
        outs["sc"].append(cbufnew.reshape(DEC_BATCH, CONV_WIDTH - 1, D_CONV))
    y_prompt = x[:N_PROMPT].reshape(BATCH, SEQ, D_MODEL)
    y_sample = x[N_PROMPT:].reshape(DEC_BATCH, 1, D_MODEL)
    stk = lambda n: jnp.stack(outs[n])
    return (y_prompt, y_sample, stk("pk"), stk("pv"), stk("ph"), stk("psc"), stk("pc"),
            stk("sk"), stk("sv"), stk("sh"), stk("ssc"), stk("sc"))
```

```python
import functools
import math

import jax
import jax.numpy as jnp
from jax import lax
from jax.experimental import pallas as pl
from jax.experimental.pallas import tpu as pltpu

D_MODEL = 1024
BATCH = 2
SEQ = 8192
DEPTH = 4
DEC_BATCH = 128
PAST_LEN = 8192
N_HEADS = 8
N_KV_HEADS = 2
HEAD_DIM = 64
Q_PER_KV = N_HEADS // N_KV_HEADS
D_ATTN = N_HEADS * HEAD_DIM
D_KV = N_KV_HEADS * HEAD_DIM
WINDOW = 128
BLK = 128
ROPE_THETA = 10000.0
SSD_HEADS = 4
SSD_HEAD_DIM = 64
D_SSD = SSD_HEADS * SSD_HEAD_DIM
SSD_GROUPS = 2
SSD_STATE = 64
SSD_CONV = 4
D_XBC = D_SSD + 2 * SSD_GROUPS * SSD_STATE
D_CONV = 256
CONV_WIDTH = 3
D_MIX = D_ATTN + D_SSD + D_CONV
D_FF = 2816
N_EXPERTS = 8
ALPHA = (2 * DEPTH) ** 0.25
NORM_EPS = 1e-5

LANES = 128
N_PROMPT = BATCH * SEQ
N_TOK = N_PROMPT + DEC_BATCH
N_BLK = SEQ // BLK
TM = 512
N_TILES = pl.cdiv(N_TOK, TM)
VMEM_LIMIT = 56 * 1024 * 1024

C_Q, C_K, C_V, C_Z, C_XBC = 0, 512, 640, 768, 1024
C_GB, C_GC, C_U, C_DT = 1536, 1792, 2048, 2304
D_PROJ = C_DT + LANES

BF16 = jnp.bfloat16
F32 = jnp.float32


def _sigmoid(x):
    return 1.0 / (1.0 + jnp.exp(-x))


def _silu(x):
    return x * _sigmoid(x)


def _softplus(x):
    return jnp.maximum(x, 0.0) + jnp.log1p(jnp.exp(-jnp.abs(x)))


def _layernorm(xf, g, b):
    mu = jnp.mean(xf, -1, keepdims=True)
    xc = xf - mu
    var = jnp.mean(xc * xc, -1, keepdims=True)
    return xc * lax.rsqrt(var + NORM_EPS) * g + b


def _rope(x, cos, sin):
    w = x.shape[1]
    reps = w // LANES
    if reps > 1:
        cos = jnp.concatenate([cos] * reps, axis=1)
        sin = jnp.concatenate([sin] * reps, axis=1)
    lane = lax.broadcasted_iota(jnp.int32, x.shape, 1)
    first_half = (lane % HEAD_DIM) < (HEAD_DIM // 2)
    partner = jnp.where(first_half, pltpu.roll(x, w - HEAD_DIM // 2, 1), pltpu.roll(x, HEAD_DIM // 2, 1))
    return x * cos + partner * sin


def _dot(a, b):
    return jnp.dot(a.astype(BF16), b.astype(BF16), preferred_element_type=F32)


def _dot_nt(a, b):
    return lax.dot_general(a.astype(BF16), b.astype(BF16), (((1,), (1,)), ((), ())), preferred_element_type=F32)


def _dot_tn(a, b):
    return lax.dot_general(a.astype(BF16), b.astype(BF16), (((0,), (0,)), ((), ())), preferred_element_type=F32)


def _inproj_kernel(x_ref, w_ref, o_ref):
    o_ref[...] = jnp.dot(x_ref[...].astype(BF16), w_ref[...], preferred_element_type=F32)


def _inproj(x, w):
    return pl.pallas_call(
        _inproj_kernel,
        grid=(N_TILES,),
        in_specs=[pl.BlockSpec((TM, D_MODEL), lambda i: (i, 0)),
                  pl.BlockSpec((D_MODEL, D_PROJ), lambda i: (0, 0))],
        out_specs=pl.BlockSpec((TM, D_PROJ), lambda i: (i, 0)),
        out_shape=jax.ShapeDtypeStruct((N_TOK, D_PROJ), F32),
        compiler_params=pltpu.CompilerParams(dimension_semantics=("arbitrary",), vmem_limit_bytes=VMEM_LIMIT),
        name="inproj",
    )(x, w)


def _ssd_gate_norm(y, z, norm_w):
    y = y * _silu(z)
    return y * lax.rsqrt(jnp.mean(y * y, -1, keepdims=True) + NORM_EPS) * norm_w


def _mixer_prompt_kernel(sinks_ref, q_ref, k_ref, v_ref, z_ref, xbc_ref, gb_ref, gc_ref, u_ref, dt_ref,
                         cos_ref, sin_ref, cw_ref, cb_ref, dtb_ref, alog_ref, dskip_ref, nw_ref, scw_ref,
                         mix_ref, klast_ref, hlast_ref, culast_ref,
                         kprev, vprev, conv_ext, cu_ext, hstate):
    j = pl.program_id(1)

    @pl.when(j == 0)
    def _():
        kprev[...] = jnp.zeros_like(kprev)
        vprev[...] = jnp.zeros_like(vprev)
        conv_ext[0:8, :] = jnp.zeros((8, D_XBC), F32)
        cu_ext[0:8, :] = jnp.zeros((8, D_CONV), F32)
        hstate[...] = jnp.zeros_like(hstate)

    cos = cos_ref[...]
    sin = sin_ref[...]
    q = _rope(q_ref[...], cos, sin) * (1.0 / math.sqrt(HEAD_DIM))
    k = _rope(k_ref[...], cos, sin)
    v = v_ref[...]
    kp = kprev[...]
    vp = vprev[...]
    rows = Q_PER_KV * BLK
    r = lax.broadcasted_iota(jnp.int32, (rows, 2 * BLK), 0) % BLK
    c = lax.broadcasted_iota(jnp.int32, (rows, 2 * BLK), 1)
    allowed = (c >= r) & (c - BLK <= r) & ((c >= BLK) | (j > 0))
    heads = []
    for g in range(N_KV_HEADS):
        sl = slice(g * HEAD_DIM, (g + 1) * HEAD_DIM)
        k_all = jnp.concatenate([kp[:, sl], k[:, sl]], axis=0)
        v_all = jnp.concatenate([vp[:, sl], v[:, sl]], axis=0)
        qs = jnp.concatenate([q[:, (g * Q_PER_KV + h) * HEAD_DIM:(g * Q_PER_KV + h + 1) * HEAD_DIM]
                              for h in range(Q_PER_KV)], axis=0)
        s = jnp.where(allowed, _dot_nt(qs, k_all), -jnp.inf)
        sink = jnp.concatenate([jnp.full((BLK, 1), sinks_ref[g * Q_PER_KV + h], F32) for h in range(Q_PER_KV)],
                               axis=0)
        m = jnp.maximum(jnp.max(s, -1, keepdims=True), sink)
        p = jnp.exp(s - m)
        denom = jnp.sum(p, -1, keepdims=True) + jnp.exp(sink - m)
        o = _dot(p, v_all) * (1.0 / denom)
        heads += [o[h * BLK:(h + 1) * BLK, :] for h in range(Q_PER_KV)]
    mix_ref[:, 0:D_ATTN] = jnp.concatenate(heads, axis=1).astype(mix_ref.dtype)
    kprev[...] = k
    vprev[...] = v
    klast_ref[0] = k

    conv_ext[8:8 + BLK, :] = xbc_ref[...]
    acc = conv_ext[5:5 + BLK, :] * cw_ref[0:1, :]
    for t in range(1, SSD_CONV):
        acc = acc + conv_ext[5 + t:5 + t + BLK, :] * cw_ref[t:t + 1, :]
    conv_ext[0:8, :] = conv_ext[BLK:BLK + 8, :]
    xbc = _silu(acc + cb_ref[...])
    xs = xbc[:, 0:D_SSD]
    bm = xbc[:, D_SSD:D_SSD + LANES]
    cm = xbc[:, D_SSD + LANES:D_SSD + 2 * LANES]
    dt = _softplus(dt_ref[...] + dtb_ref[...])
    dta = dt * (-jnp.exp(alog_ref[...]))
    ti = lax.broadcasted_iota(jnp.int32, (BLK, BLK), 0)
    si = lax.broadcasted_iota(jnp.int32, (BLK, BLK), 1)
    causal = si <= ti
    cum = jnp.dot(causal.astype(F32), dta, preferred_element_type=F32, precision=lax.Precision.HIGHEST)
    cum_t = cum.T
    ys = []
    for g in range(SSD_GROUPS):
        sl = slice(g * SSD_STATE, (g + 1) * SSD_STATE)
        b_g = bm[:, sl]
        c_g = cm[:, sl]
        cb_scores = _dot_nt(c_g, b_g)
        for hh in range(SSD_HEADS // SSD_GROUPS):
            h = g * (SSD_HEADS // SSD_GROUPS) + hh
            ccol = cum[:, h:h + 1]
            crow = cum_t[h:h + 1, :]
            clast = cum[BLK - 1:BLK, h:h + 1]
            decay = jnp.exp(jnp.where(causal, ccol - crow, -jnp.inf))
            x_h = xs[:, h * SSD_HEAD_DIM:(h + 1) * SSD_HEAD_DIM]
            xdt = x_h * dt[:, h:h + 1]
            h_prev = hstate[h]
            y = _dot(cb_scores * decay, xdt)
            y = y + _dot_nt(c_g, h_prev) * jnp.exp(ccol)
            states = _dot_tn(xdt * jnp.exp(clast - ccol), b_g)
            hstate[h] = h_prev * jnp.exp(clast) + states
            ys.append(y)
    y = jnp.concatenate(ys, axis=1) + dskip_ref[...] * xs
    mix_ref[:, D_ATTN:D_ATTN + D_SSD] = _ssd_gate_norm(y, z_ref[...], nw_ref[...]).astype(mix_ref.dtype)
    hlast_ref[0] = hstate[...]

    cu_ext[8:8 + BLK, :] = gc_ref[...] * u_ref[...]
    cc = cu_ext[6:6 + BLK, :] * scw_ref[0:1, :]
    for t in range(1, CONV_WIDTH):
        cc = cc + cu_ext[6 + t:6 + t + BLK, :] * scw_ref[t:t + 1, :]
    tail = cu_ext[BLK:BLK + 8, :]
    cu_ext[0:8, :] = tail
    culast_ref[0] = tail
    mix_ref[:, D_ATTN + D_SSD:D_MIX] = (gb_ref[...] * cc).astype(mix_ref.dtype)


def _mixer_prompt(proj, sinks, cos_tab, sin_tab, cw, cb, dtb, alog, dskip, nw, scw):
    def pblk(width, col):
        return pl.BlockSpec((BLK, width), lambda b, j, *_: (b * N_BLK + j, col // width))

    def full(a):
        return pl.BlockSpec(a.shape, lambda b, j, *_: (0,) * a.ndim)

    params = (cw, cb, dtb, alog, dskip, nw, scw)
    grid_spec = pltpu.PrefetchScalarGridSpec(
        num_scalar_prefetch=1,
        grid=(BATCH, N_BLK),
        in_specs=[pblk(D_ATTN, C_Q), pblk(D_KV, C_K), pblk(D_KV, C_V), pblk(D_SSD, C_Z), pblk(D_XBC, C_XBC),
                  pblk(D_CONV, C_GB), pblk(D_CONV, C_GC), pblk(D_CONV, C_U), pblk(LANES, C_DT),
                  pl.BlockSpec((BLK, LANES), lambda b, j, *_: (j, 0)),
                  pl.BlockSpec((BLK, LANES), lambda b, j, *_: (j, 0))] + [full(a) for a in params],
        out_specs=[pl.BlockSpec((BLK, D_MIX), lambda b, j, *_: (b * N_BLK + j, 0)),
                   pl.BlockSpec((1, BLK, D_KV), lambda b, j, *_: (b, 0, 0)),
                   pl.BlockSpec((1, SSD_HEADS, SSD_HEAD_DIM, SSD_STATE), lambda b, j, *_: (b, 0, 0, 0)),
                   pl.BlockSpec((1, 8, D_CONV), lambda b, j, *_: (b, 0, 0))],
        scratch_shapes=[pltpu.VMEM((BLK, D_KV), F32), pltpu.VMEM((BLK, D_KV), F32),
                        pltpu.VMEM((BLK + 8, D_XBC), F32), pltpu.VMEM((BLK + 8, D_CONV), F32),
                        pltpu.VMEM((SSD_HEADS, SSD_HEAD_DIM, SSD_STATE), F32)],
    )
    return pl.pallas_call(
        _mixer_prompt_kernel,
        grid_spec=grid_spec,
        out_shape=[jax.ShapeDtypeStruct((N_PROMPT, D_MIX), BF16),
                   jax.ShapeDtypeStruct((BATCH, BLK, D_KV), F32),
                   jax.ShapeDtypeStruct((BATCH, SSD_HEADS, SSD_HEAD_DIM, SSD_STATE), F32),
                   jax.ShapeDtypeStruct((BATCH, 8, D_CONV), F32)],
        compiler_params=pltpu.CompilerParams(dimension_semantics=("arbitrary", "arbitrary"),
                                             vmem_limit_bytes=VMEM_LIMIT),
        name="mixer_prompt",
    )(sinks, *([proj] * 9), cos_tab, sin_tab, *params)


DEC_BB = 32


def _dec_attn_kernel(q_ref, k_ref, v_ref, kc_ref, vc_ref, cos_ref, sin_ref, sink_ref,
                     o_ref, knew_ref, qbd, kn8, vn8, o_scr):
    cos = cos_ref[...]
    sin = sin_ref[...]
    q = _rope(q_ref[...], cos, sin) * (1.0 / math.sqrt(HEAD_DIM))
    kn = _rope(k_ref[...], cos, sin)
    vn = v_ref[...]
    knew_ref[...] = kn
    lane = lax.broadcasted_iota(jnp.int32, (DEC_BB, LANES), 1)
    for h in range(N_HEADS):
        g = h // Q_PER_KV
        pair = q[:, (h // 2) * LANES:(h // 2 + 1) * LANES]
        if h % 2 != g:
            pair = pltpu.roll(pair, HEAD_DIM, 1)
        qbd[pl.ds(h, DEC_BB, stride=8), :] = jnp.where(lane // HEAD_DIM == g, pair, 0.0)
        kn8[pl.ds(h, DEC_BB, stride=8), :] = kn
        vn8[pl.ds(h, DEC_BB, stride=8), :] = vn
    q3 = qbd[...].reshape(DEC_BB, N_HEADS, LANES)
    kn3 = kn8[...].reshape(DEC_BB, N_HEADS, LANES)
    vn3 = vn8[...].reshape(DEC_BB, N_HEADS, LANES)
    s3 = jnp.einsum('bhl,bsl->bhs', q3.astype(BF16), kc_ref[...].astype(BF16), preferred_element_type=F32)
    s_self = jnp.sum(q3 * kn3, axis=-1, keepdims=True)
    sink3 = sink_ref[...][None, :, 0:1]
    m = jnp.maximum(jnp.maximum(jnp.max(s3, -1, keepdims=True), s_self), sink3)
    p3 = jnp.exp(s3 - m)
    p_self = jnp.exp(s_self - m)
    denom = jnp.sum(p3, -1, keepdims=True) + p_self + jnp.exp(sink3 - m)
    o3 = jnp.einsum('bhs,bsl->bhl', p3.astype(BF16), vc_ref[...].astype(BF16), preferred_element_type=F32)
    o3 = (o3 + p_self * vn3) * (1.0 / denom)
    h3 = lax.broadcasted_iota(jnp.int32, o3.shape, 1)
    l3 = lax.broadcasted_iota(jnp.int32, o3.shape, 2)
    o3 = jnp.where(l3 // HEAD_DIM == h3 // Q_PER_KV, o3, 0.0)
    o_scr[...] = o3.reshape(DEC_BB * N_HEADS, LANES)
    for pair in range(N_HEADS // 2):
        g = (2 * pair) // Q_PER_KV
        a = o_scr[pl.ds(2 * pair, DEC_BB, stride=8), :]
        b = o_scr[pl.ds(2 * pair + 1, DEC_BB, stride=8), :]
        if g == 1:
            a = pltpu.roll(a, HEAD_DIM, 1)
        else:
            b = pltpu.roll(b, HEAD_DIM, 1)
        o_ref[:, pair * LANES:(pair + 1) * LANES] = a + b


def _dec_attn(proj, kc, vc, cos1, sin1, sink_rows):
    base = N_PROMPT // DEC_BB

    def pblk(width, col):
        return pl.BlockSpec((DEC_BB, width), lambda i: (base + i, col // width))

    def full(a):
        return pl.BlockSpec(a.shape, lambda i: (0,) * a.ndim)

    return pl.pallas_call(
        _dec_attn_kernel,
        grid=(DEC_BATCH // DEC_BB,),
        in_specs=[pblk(D_ATTN, C_Q), pblk(D_KV, C_K), pblk(D_KV, C_V),
                  pl.BlockSpec((DEC_BB, WINDOW, D_KV), lambda i: (i, 0, 0)),
                  pl.BlockSpec((DEC_BB, WINDOW, D_KV), lambda i: (i, 0, 0)),
                  full(cos1), full(sin1), full(sink_rows)],
        out_specs=[pl.BlockSpec((DEC_BB, D_ATTN), lambda i: (i, 0)),
                   pl.BlockSpec((DEC_BB, D_KV), lambda i: (i, 0))],
        out_shape=[jax.ShapeDtypeStruct((DEC_BATCH, D_ATTN), F32),
                   jax.ShapeDtypeStruct((DEC_BATCH, D_KV), F32)],
        scratch_shapes=[pltpu.VMEM((DEC_BB * N_HEADS, LANES), F32)] * 4,
        compiler_params=pltpu.CompilerParams(dimension_semantics=("arbitrary",), vmem_limit_bytes=VMEM_LIMIT),
        name="dec_attn",
    )(proj, proj, proj, kc, vc, cos1, sin1, sink_rows)


D_STATE_FLAT = SSD_HEAD_DIM * SSD_STATE


def _dec_ssd_kernel(oattn_ref, z_ref, xbc_ref, gb_ref, gc_ref, u_ref, dt_ref, sbuf_ref, cbuf_ref, st_ref,
                    cw_ref, cb_ref, dtb_ref, alog_ref, dskip_ref, nw_ref, scw_ref,
                    mix_ref, stnew_ref, sbufnew_ref, cbufnew_ref,
                    st_t, stnew_t, xdt_t, b_t, c_t, dec_t, y_t):
    xbc_new = xbc_ref[...]
    acc = sbuf_ref[:, 0:D_XBC] * cw_ref[0:1, :]
    for t in range(1, SSD_CONV - 1):
        acc = acc + sbuf_ref[:, t * D_XBC:(t + 1) * D_XBC] * cw_ref[t:t + 1, :]
    acc = acc + xbc_new * cw_ref[SSD_CONV - 1:SSD_CONV, :]
    sbufnew_ref[:, 0:(SSD_CONV - 2) * D_XBC] = sbuf_ref[:, D_XBC:(SSD_CONV - 1) * D_XBC]
    sbufnew_ref[:, (SSD_CONV - 2) * D_XBC:(SSD_CONV - 1) * D_XBC] = xbc_new
    xbc = _silu(acc + cb_ref[...])
    xs = xbc[:, 0:D_SSD]
    dt = _softplus(dt_ref[...] + dtb_ref[...])
    dta = dt * (-jnp.exp(alog_ref[...]))
    dec_t[...] = jnp.exp(dta).T
    dt_t = dt.T
    xs_t = xs.T
    for h in range(SSD_HEADS):
        sl = slice(h * SSD_HEAD_DIM, (h + 1) * SSD_HEAD_DIM)
        xdt_t[sl, :] = xs_t[sl, :] * dt_t[h:h + 1, :]
    b_t[...] = xbc[:, D_SSD:D_SSD + LANES].T
    c_t[...] = xbc[:, D_SSD + LANES:D_SSD + 2 * LANES].T
    for h in range(SSD_HEADS):
        g = h // (SSD_HEADS // SSD_GROUPS)
        st_t[...] = st_ref[:, h * D_STATE_FLAT:(h + 1) * D_STATE_FLAT].T
        dec = dec_t[h:h + 1, :]
        bg = b_t[g * SSD_STATE:(g + 1) * SSD_STATE, :]
        cg = c_t[g * SSD_STATE:(g + 1) * SSD_STATE, :]

        def body(p, carry, h=h, dec=dec, bg=bg, cg=cg):
            off = pl.multiple_of(p * SSD_STATE, SSD_STATE)
            new = st_t[pl.ds(off, SSD_STATE), :] * dec + xdt_t[pl.ds(h * SSD_HEAD_DIM + p, 1), :] * bg
            stnew_t[pl.ds(off, SSD_STATE), :] = new
            y_t[pl.ds(h * SSD_HEAD_DIM + p, 1), :] = jnp.sum(new * cg, axis=0, keepdims=True)
            return carry

        lax.fori_loop(0, SSD_HEAD_DIM, body, 0)
        stnew_ref[:, h * D_STATE_FLAT:(h + 1) * D_STATE_FLAT] = stnew_t[...].T
    y = y_t[...].T + dskip_ref[...] * xs
    mix_ref[:, 0:D_ATTN] = oattn_ref[...].astype(mix_ref.dtype)
    mix_ref[:, D_ATTN:D_ATTN + D_SSD] = _ssd_gate_norm(y, z_ref[...], nw_ref[...]).astype(mix_ref.dtype)

    cu = gc_ref[...] * u_ref[...]
    cc = cbuf_ref[:, 0:D_CONV] * scw_ref[0:1, :] + cbuf_ref[:, D_CONV:2 * D_CONV] * scw_ref[1:2, :]
    cc = cc + cu * scw_ref[2:3, :]
    cbufnew_ref[:, 0:D_CONV] = cbuf_ref[:, D_CONV:2 * D_CONV]
    cbufnew_ref[:, D_CONV:2 * D_CONV] = cu
    mix_ref[:, D_ATTN + D_SSD:D_MIX] = (gb_ref[...] * cc).astype(mix_ref.dtype)


def _dec_ssd(oattn, proj, sbuf, cbuf, st, cw, cb, dtb, alog, dskip, nw, scw):
    base = N_PROMPT // DEC_BATCH

    def pblk(width, col):
        return pl.BlockSpec((DEC_BATCH, width), lambda i: (base, col // width))

    def full(a):
        return pl.BlockSpec(a.shape, lambda i: (0,) * a.ndim)

    params = (cw, cb, dtb, alog, dskip, nw, scw)
    return pl.pallas_call(
        _dec_ssd_kernel,
        grid=(1,),
        in_specs=[full(oattn), pblk(D_SSD, C_Z), pblk(D_XBC, C_XBC), pblk(D_CONV, C_GB), pblk(D_CONV, C_GC),
                  pblk(D_CONV, C_U), pblk(LANES, C_DT), full(sbuf), full(cbuf), full(st)]
                 + [full(a) for a in params],
        out_specs=[pl.BlockSpec((DEC_BATCH, D_MIX), lambda i: (0, 0)), full(st), full(sbuf), full(cbuf)],
        out_shape=[jax.ShapeDtypeStruct((DEC_BATCH, D_MIX), BF16), jax.ShapeDtypeStruct(st.shape, F32),
                   jax.ShapeDtypeStruct(sbuf.shape, F32), jax.ShapeDtypeStruct(cbuf.shape, F32)],
        scratch_shapes=[pltpu.VMEM((D_STATE_FLAT, DEC_BATCH), F32), pltpu.VMEM((D_STATE_FLAT, DEC_BATCH), F32),
                        pltpu.VMEM((D_SSD, DEC_BATCH), F32), pltpu.VMEM((LANES, DEC_BATCH), F32),
                        pltpu.VMEM((LANES, DEC_BATCH), F32), pltpu.VMEM((LANES, DEC_BATCH), F32),
                        pltpu.VMEM((D_SSD, DEC_BATCH), F32)],
        compiler_params=pltpu.CompilerParams(dimension_semantics=("arbitrary",), vmem_limit_bytes=VMEM_LIMIT),
        name="dec_ssd",
    )(oattn, proj, proj, proj, proj, proj, proj, sbuf, cbuf, st, *params)


N_PROMPT_TILES = N_PROMPT // TM


def _outproj_kernel(mixp_ref, mixd_ref, x_ref, w_ref, g_ref, b_ref, o_ref):
    i = pl.program_id(0)

    @pl.when(i < N_PROMPT_TILES)
    def _():
        m = jnp.dot(mixp_ref[...], w_ref[...], preferred_element_type=F32)
        o_ref[...] = _layernorm(ALPHA * x_ref[...] + m, g_ref[...], b_ref[...])

    @pl.when(i == N_PROMPT_TILES)
    def _():
        m = jnp.dot(mixd_ref[...], w_ref[...], preferred_element_type=F32)
        o_ref[0:DEC_BATCH, :] = _layernorm(ALPHA * x_ref[0:DEC_BATCH, :] + m, g_ref[...], b_ref[...])


def _outproj_ln(mix_p, mix_d, x, w, g, b):
    row = lambda i: (i, 0)
    const = lambda i: (0, 0)
    return pl.pallas_call(
        _outproj_kernel,
        grid=(N_TILES,),
        in_specs=[pl.BlockSpec((TM, D_MIX), lambda i: (jnp.minimum(i, N_PROMPT_TILES - 1), 0)),
                  pl.BlockSpec((DEC_BATCH, D_MIX), const), pl.BlockSpec((TM, D_MODEL), row),
                  pl.BlockSpec((D_MIX, D_MODEL), const), pl.BlockSpec((1, D_MODEL), const),
                  pl.BlockSpec((1, D_MODEL), const)],
        out_specs=pl.BlockSpec((TM, D_MODEL), row),
        out_shape=jax.ShapeDtypeStruct((N_TOK, D_MODEL), F32),
        compiler_params=pltpu.CompilerParams(dimension_semantics=("arbitrary",), vmem_limit_bytes=VMEM_LIMIT),
        name="outproj_ln",
    )(mix_p, mix_d, x, w, g, b)


def _swiglu(xb, wg, wu, wd):
    gate = jnp.dot(xb, wg, preferred_element_type=F32)
    up = jnp.dot(xb, wu, preferred_element_type=F32)
    return jnp.dot((_silu(gate) * up).astype(BF16), wd, preferred_element_type=F32)


def _ffn_kernel(x_ref, wg_ref, wu_ref, wd_ref, g_ref, b_ref, o_ref):
    x = x_ref[...]
    f = _swiglu(x.astype(BF16), wg_ref[...], wu_ref[...], wd_ref[...])
    o_ref[...] = _layernorm(ALPHA * x + f, g_ref[...], b_ref[...])


def _ffn_ln(x, wg, wu, wd, g, b):
    row = lambda i: (i, 0)
    const = lambda i: (0, 0)
    whole = pl.BlockSpec(memory_space=pltpu.VMEM)
    return pl.pallas_call(
        _ffn_kernel,
        grid=(N_TILES,),
        in_specs=[pl.BlockSpec((TM, D_MODEL), row), whole, whole, whole,
                  pl.BlockSpec((1, D_MODEL), const), pl.BlockSpec((1, D_MODEL), const)],
        out_specs=pl.BlockSpec((TM, D_MODEL), row),
        out_shape=jax.ShapeDtypeStruct((N_TOK, D_MODEL), F32),
        compiler_params=pltpu.CompilerParams(dimension_semantics=("arbitrary",), vmem_limit_bytes=VMEM_LIMIT),
        name="ffn_ln",
    )(x, wg, wu, wd, g, b)


TOP_K = 2
TMOE = 512
MOE_TILES = (TOP_K * N_TOK) // TMOE + N_EXPERTS
MOE_ROWS = MOE_TILES * TMOE


def _router_kernel(x_ref, r_ref, route_ref, gate_ref, cnt_ref, base):
    i = pl.program_id(0)

    @pl.when(i == 0)
    def _():
        base[...] = jnp.zeros_like(base)

    lane = lax.broadcasted_iota(jnp.int32, (TM, LANES), 1)
    row = lax.broadcasted_iota(jnp.int32, (TM, LANES), 0) + i * TM
    valid = row < N_TOK
    rows_ok = lax.broadcasted_iota(jnp.int32, (TM, D_MODEL), 0) + i * TM < N_TOK
    x = jnp.where(rows_ok, x_ref[...], 0.0)
    logits = jnp.dot(x, r_ref[...], preferred_element_type=F32, precision=lax.Precision.HIGHEST)
    logits = jnp.where(lane < N_EXPERTS, logits, -jnp.inf)
    v1 = jnp.max(logits, -1, keepdims=True)
    i1 = jnp.min(jnp.where(logits == v1, lane, LANES), -1, keepdims=True)
    rest = jnp.where(lane == i1, -jnp.inf, logits)
    v2 = jnp.max(rest, -1, keepdims=True)
    i2 = jnp.min(jnp.where(rest == v2, lane, LANES), -1, keepdims=True)
    e2 = jnp.exp(v2 - v1)
    g1 = 1.0 / (1.0 + e2)
    oh1 = jnp.where((lane == i1) & valid, 1.0, 0.0)
    oh2 = jnp.where((lane == i2) & valid, 1.0, 0.0)
    tr = lax.broadcasted_iota(jnp.int32, (TM, TM), 0)
    tc = lax.broadcasted_iota(jnp.int32, (TM, TM), 1)
    before = jnp.where(tc < tr, 1.0, 0.0).astype(BF16)
    cum1 = jnp.dot(before, oh1.astype(BF16), preferred_element_type=F32)
    cum2 = jnp.dot(before, oh2.astype(BF16), preferred_element_type=F32)
    tot1 = jnp.sum(oh1, axis=0, keepdims=True)
    tot2 = jnp.sum(oh2, axis=0, keepdims=True)
    b = base[...]
    rank1 = jnp.sum(oh1 * (cum1 + b), -1, keepdims=True).astype(jnp.int32)
    rank2 = jnp.sum(oh2 * (cum2 + b + tot1), -1, keepdims=True).astype(jnp.int32)
    total = b + tot1 + tot2
    base[...] = total
    route_ref[...] = jnp.where(lane == 0, i1, jnp.where(lane == 1, i2, jnp.where(lane == 2, rank1,
                                                                                 jnp.where(lane == 3, rank2, 0))))
    gate_ref[...] = jnp.where(lane == 0, g1, jnp.where(lane == 1, e2 * g1, 0.0))
    cnt_ref[...] = jnp.broadcast_to(total, cnt_ref.shape).astype(jnp.int32)


def _router(x, router):
    row = lambda i: (i, 0)
    const = lambda i: (0, 0)
    return pl.pallas_call(
        _router_kernel,
        grid=(N_TILES,),
        in_specs=[pl.BlockSpec((TM, D_MODEL), row), pl.BlockSpec((D_MODEL, LANES), const)],
        out_specs=[pl.BlockSpec((TM, LANES), row), pl.BlockSpec((TM, LANES), row), pl.BlockSpec((8, LANES), const)],
        out_shape=[jax.ShapeDtypeStruct((N_TOK, LANES), jnp.int32), jax.ShapeDtypeStruct((N_TOK, LANES), F32),
                   jax.ShapeDtypeStruct((8, LANES), jnp.int32)],
        scratch_shapes=[pltpu.VMEM((1, LANES), F32)],
        compiler_params=pltpu.CompilerParams(dimension_semantics=("arbitrary",), vmem_limit_bytes=VMEM_LIMIT),
        name="router",
    )(x, router)


def _row_copy(src, src_row, dst, dst_row, sem):
    return pltpu.make_async_copy(src.at[pl.ds(src_row, 1)], dst.at[pl.ds(dst_row, 1)], sem)


def _dispatch_kernel(pos1_ref, pos2_ref, cnt_ref, off_ref, pad_ref, x_hbm, xs_hbm, sem):
    i = pl.program_id(0)
    start = i * TM
    n = jnp.minimum(TM, N_TOK - start)

    def issue(r, carry):
        t = start + r
        _row_copy(x_hbm, t, xs_hbm, pos1_ref[t], sem).start()
        _row_copy(x_hbm, t, xs_hbm, pos2_ref[t], sem).start()
        return carry

    lax.fori_loop(0, n, issue, 0)

    def drain(r, carry):
        _row_copy(x_hbm, 0, xs_hbm, 0, sem).wait()
        return carry

    lax.fori_loop(0, TOP_K * n, drain, 0)

    @pl.when(i == N_TILES - 1)
    def _():
        for e in range(N_EXPERTS):
            lo = off_ref[e] + cnt_ref[e]

            def fill(q, carry, lo=lo):
                _row_copy(x_hbm, 0, xs_hbm, lo + q, sem).start()
                return carry

            lax.fori_loop(0, pad_ref[e], fill, 0)
            lax.fori_loop(0, pad_ref[e], drain, 0)


def _dispatch(x, pos1, pos2, counts, offs, pads):
    return pl.pallas_call(
        _dispatch_kernel,
        grid_spec=pltpu.PrefetchScalarGridSpec(
            num_scalar_prefetch=5, grid=(N_TILES,),
            in_specs=[pl.BlockSpec(memory_space=pl.ANY)],
            out_specs=pl.BlockSpec(memory_space=pl.ANY),
            scratch_shapes=[pltpu.SemaphoreType.DMA(())]),
        out_shape=jax.ShapeDtypeStruct((MOE_ROWS, D_MODEL), F32),
        compiler_params=pltpu.CompilerParams(dimension_semantics=("arbitrary",)),
        name="moe_dispatch",
    )(pos1, pos2, counts, offs, pads, x)


def _gffn_kernel(te_ref, nv_ref, xs_ref, wg_ref, wu_ref, wd_ref, ys_ref):
    @pl.when(pl.program_id(0) < nv_ref[0])
    def _():
        ys_ref[...] = _swiglu(xs_ref[...].astype(BF16), wg_ref[0], wu_ref[0], wd_ref[0])


def _gffn(xs, tile_expert, n_valid, wg, wu, wd):
    def tile(i, te, nv):
        return (jnp.minimum(i, nv[0] - 1), 0)

    def expert(i, te, nv):
        return (te[jnp.minimum(i, nv[0] - 1)], 0, 0)

    return pl.pallas_call(
        _gffn_kernel,
        grid_spec=pltpu.PrefetchScalarGridSpec(
            num_scalar_prefetch=2, grid=(MOE_TILES,),
            in_specs=[pl.BlockSpec((TMOE, D_MODEL), tile), pl.BlockSpec((1, D_MODEL, D_FF), expert),
                      pl.BlockSpec((1, D_MODEL, D_FF), expert), pl.BlockSpec((1, D_FF, D_MODEL), expert)],
            out_specs=pl.BlockSpec((TMOE, D_MODEL), tile)),
        out_shape=jax.ShapeDtypeStruct((MOE_ROWS, D_MODEL), F32),
        compiler_params=pltpu.CompilerParams(dimension_semantics=("arbitrary",), vmem_limit_bytes=VMEM_LIMIT),
        name="moe_ffn",
    )(tile_expert, n_valid, xs, wg, wu, wd)


def _combine_kernel(pos1_ref, pos2_ref, x_ref, gate_ref, ys_hbm, g_ref, b_ref, o_ref, rows, sem):
    start = pl.program_id(0) * TM

    def issue(r, carry):
        t = jnp.minimum(start + r, N_TOK - 1)
        pltpu.make_async_copy(ys_hbm.at[pl.ds(pos1_ref[t], 1)], rows.at[0, pl.ds(r, 1)], sem).start()
        pltpu.make_async_copy(ys_hbm.at[pl.ds(pos2_ref[t], 1)], rows.at[1, pl.ds(r, 1)], sem).start()
        return carry

    lax.fori_loop(0, TM, issue, 0)
    for s in range(TOP_K):
        pltpu.make_async_copy(ys_hbm.at[pl.ds(0, TM)], rows.at[s], sem).wait()
    gate = gate_ref[...]
    f = gate[:, 0:1] * rows[0] + gate[:, 1:2] * rows[1]
    o_ref[...] = _layernorm(ALPHA * x_ref[...] + f, g_ref[...], b_ref[...])


def _combine_ln(x, gates, ys, pos1, pos2, g, b):
    row = lambda i, *_: (i, 0)
    const = lambda i, *_: (0, 0)
    return pl.pallas_call(
        _combine_kernel,
        grid_spec=pltpu.PrefetchScalarGridSpec(
            num_scalar_prefetch=2, grid=(N_TILES,),
            in_specs=[pl.BlockSpec((TM, D_MODEL), row), pl.BlockSpec((TM, LANES), row),
                      pl.BlockSpec(memory_space=pl.ANY),
                      pl.BlockSpec((1, D_MODEL), const), pl.BlockSpec((1, D_MODEL), const)],
            out_specs=pl.BlockSpec((TM, D_MODEL), row),
            scratch_shapes=[pltpu.VMEM((TOP_K, TM, D_MODEL), F32), pltpu.SemaphoreType.DMA(())]),
        out_shape=jax.ShapeDtypeStruct((N_TOK, D_MODEL), F32),
        compiler_params=pltpu.CompilerParams(dimension_semantics=("arbitrary",), vmem_limit_bytes=VMEM_LIMIT),
        name="moe_combine_ln",
    )(pos1, pos2, x, gates, ys, g, b)


def _moe_ln(x, router, wg, wu, wd, g, b):
    route, gates, cnt = _router(x, router)
    counts = cnt[0, :N_EXPERTS]
    padded = (counts + TMOE - 1) // TMOE * TMOE
    ends = jnp.cumsum(padded)
    offs = ends - padded
    pos1 = jnp.take(offs, route[:, 0]) + route[:, 2]
    pos2 = jnp.take(offs, route[:, 1]) + route[:, 3]
    tile_ends = ends // TMOE
    tile_ids = jnp.arange(MOE_TILES, dtype=jnp.int32)
    tile_expert = jnp.minimum(jnp.sum(tile_ids[:, None] >= tile_ends[None, :], axis=1), N_EXPERTS - 1)
    xs = _dispatch(x, pos1, pos2, counts, offs, padded - counts)
    ys = _gffn(xs, tile_expert.astype(jnp.int32), tile_ends[N_EXPERTS - 1:], wg, wu, wd)
    return _combine_ln(x, gates, ys, pos1, pos2, g, b)


def _rope_tables(pos):
    half = HEAD_DIM // 2
    inv_freq = ROPE_THETA ** (-jnp.arange(half, dtype=F32) / half)
    ang = pos.astype(F32)[:, None] * inv_freq[None, :]
    cos = jnp.cos(ang)
    sin = jnp.sin(ang)
    return jnp.concatenate([cos] * 4, axis=1), jnp.concatenate([-sin, sin, -sin, sin], axis=1)


def _pad_lanes(a):
    return jnp.pad(a, [(0, 0)] * (a.ndim - 1) + [(0, LANES - a.shape[-1])])


def kernel(x_prompt, x_sample, cache_win_k, cache_win_v, state_ssd, state_ssd_conv, state_conv, w_in, attn_sinks,
           ssd_conv_w, ssd_conv_b, ssd_dt_bias, ssd_a_log, ssd_d, ssd_norm_w, sconv_w, w_out, ln1_g, ln1_b, ln2_g,
           ln2_b, ffn_w_gate, ffn_w_up, ffn_w_down, moe_router, moe_w_gate, moe_w_up, moe_w_down):
    x = jnp.concatenate([x_prompt.reshape(N_PROMPT, D_MODEL), x_sample.reshape(DEC_BATCH, D_MODEL)], axis=0)
    cos_p, sin_p = _rope_tables(jnp.arange(SEQ, dtype=jnp.int32))
    cos_s, sin_s = _rope_tables(PAST_LEN + jnp.arange(1, dtype=jnp.int32))
    split = C_GB + SSD_HEADS
    w_in_r = jnp.concatenate([w_in[:, :, :C_GB], w_in[:, :, split:], _pad_lanes(w_in[:, :, C_GB:split])],
                             axis=-1).astype(BF16)
    w_out_b = w_out.astype(BF16)
    dtb = _pad_lanes(ssd_dt_bias)[:, None, :]
    alog = _pad_lanes(ssd_a_log)[:, None, :]
    dskip = jnp.repeat(ssd_d, SSD_HEAD_DIM, axis=-1)[:, None, :]
    router = _pad_lanes(moe_router)
    sink_rows = jnp.broadcast_to(attn_sinks[:, :, None], (DEPTH, N_HEADS, LANES))
    kc = cache_win_k.reshape(DEPTH, DEC_BATCH, WINDOW, D_KV)
    vc = cache_win_v.reshape(DEPTH, DEC_BATCH, WINDOW, D_KV)
    st = state_ssd.reshape(DEPTH, DEC_BATCH, SSD_HEADS * D_STATE_FLAT)
    sbuf = state_ssd_conv.reshape(DEPTH, DEC_BATCH, (SSD_CONV - 1) * D_XBC)
    cbuf = state_conv.reshape(DEPTH, DEC_BATCH, (CONV_WIDTH - 1) * D_CONV)

    outs = {n: [] for n in ("pk", "pv", "ph", "psc", "pc", "sk", "sv", "sh", "ssc", "sc")}
    for i in range(DEPTH):
        proj = _inproj(x, w_in_r[i])
        params = (ssd_conv_w[i], ssd_conv_b[i][None, :], dtb[i], alog[i], dskip[i], ssd_norm_w[i][None, :],
                  sconv_w[i])
        mix_p, klast, hlast, culast = _mixer_prompt(proj, attn_sinks[i], cos_p, sin_p, *params)
        oattn, knew = _dec_attn(proj, kc[i], vc[i], cos_s, sin_s, sink_rows[i])
        mix_d, stnew, sbufnew, cbufnew = _dec_ssd(oattn, proj, sbuf[i], cbuf[i], st[i], *params)
        x = _outproj_ln(mix_p, mix_d, x, w_out_b[i], ln1_g[i][None, :], ln1_b[i][None, :])
        j = i // 2
        if i % 2 == 0:
            x = _ffn_ln(x, ffn_w_gate[j].astype(BF16), ffn_w_up[j].astype(BF16), ffn_w_down[j].astype(BF16),
                        ln2_g[i][None, :], ln2_b[i][None, :])
        else:
            x = _moe_ln(x, router[j], moe_w_gate[j].astype(BF16), moe_w_up[j].astype(BF16),
                        moe_w_down[j].astype(BF16), ln2_g[i][None, :], ln2_b[i][None, :])
        def tail(n_rows, col, width):
            return jnp.stack([proj[(b + 1) * SEQ - n_rows:(b + 1) * SEQ, col:col + width] for b in range(BATCH)])

        outs["pk"].append(klast.reshape(BATCH, WINDOW, N_KV_HEADS, HEAD_DIM))
        outs["pv"].append(tail(WINDOW, C_V, D_KV).reshape(BATCH, WINDOW, N_KV_HEADS, HEAD_DIM))
        outs["ph"].append(hlast)
        outs["psc"].append(tail(SSD_CONV - 1, C_XBC, D_XBC))
        outs["pc"].append(culast[:, 8 - (CONV_WIDTH - 1):, :])
        vnew = proj[N_PROMPT:, C_V:C_V + D_KV]
        outs["sk"].append(jnp.concatenate([kc[i][:, 1:], knew[:, None, :]], axis=1)
                          .reshape(DEC_BATCH, WINDOW, N_KV_HEADS, HEAD_DIM))
        outs["sv"].append(jnp.concatenate([vc[i][:, 1:], vnew[:, None, :]], axis=1)
                          .reshape(DEC_BATCH, WINDOW, N_KV_HEADS, HEAD_DIM))
        outs["sh"].append(stnew.reshape(DEC_BATCH, SSD_HEADS, SSD_HEAD_DIM, SSD_STATE))
        outs["ssc"].append(sbufnew.reshape(DEC_BATCH, SSD_CONV - 1, D_XBC))
        outs["sc"].append(cbufnew.reshape(DEC_BATCH, CONV_WIDTH - 1, D_CONV))
    y_prompt = x[:N_PROMPT].reshape(BATCH, SEQ, D_MODEL)
    y_sample = x[N_PROMPT:].reshape(DEC_BATCH, 1, D_MODEL)
    stk = lambda n: jnp.stack(outs[n])
    return (y_prompt, y_sample, stk("pk"), stk("pv"), stk("ph"), stk("psc"), stk("pc"),
            stk("sk"), stk("sv"), stk("sh"), stk("ssc"), stk("sc"))
```

```python
import functools
import math

import jax
import jax.numpy as jnp
from jax import lax
from jax.experimental import pallas as pl
from jax.experimental.pallas import tpu as pltpu

D_MODEL = 1024
BATCH = 2
SEQ = 8192
DEPTH = 4
DEC_BATCH = 128
PAST_LEN = 8192
N_HEADS = 8
N_KV_HEADS = 2
HEAD_DIM = 64
Q_PER_KV = N_HEADS // N_KV_HEADS
D_ATTN = N_HEADS * HEAD_DIM
D_KV = N_KV_HEADS * HEAD_DIM
WINDOW = 128
BLK = 128
ROPE_THETA = 10000.0
SSD_HEADS = 4
SSD_HEAD_DIM = 64
D_SSD = SSD_HEADS * SSD_HEAD_DIM
SSD_GROUPS = 2
SSD_STATE = 64
SSD_CONV = 4
D_XBC = D_SSD + 2 * SSD_GROUPS * SSD_STATE
D_CONV = 256
CONV_WIDTH = 3
D_MIX = D_ATTN + D_SSD + D_CONV
D_FF = 2816
N_EXPERTS = 8
ALPHA = (2 * DEPTH) ** 0.25
NORM_EPS = 1e-5

LANES = 128
N_PROMPT = BATCH * SEQ
N_TOK = N_PROMPT + DEC_BATCH
N_BLK = SEQ // BLK
TM = 512
N_TILES = pl.cdiv(N_TOK, TM)
VMEM_LIMIT = 56 * 1024 * 1024

C_Q, C_K, C_V, C_Z, C_XBC = 0, 512, 640, 768, 1024
C_GB, C_GC, C_U, C_DT = 1536, 1792, 2048, 2304
D_PROJ = C_DT + LANES

BF16 = jnp.bfloat16
F32 = jnp.float32


def _sigmoid(x):
    return 1.0 / (1.0 + jnp.exp(-x))


def _silu(x):
    return x * _sigmoid(x)


def _softplus(x):
    return jnp.maximum(x, 0.0) + jnp.log1p(jnp.exp(-jnp.abs(x)))


def _layernorm(xf, g, b):
    mu = jnp.mean(xf, -1, keepdims=True)
    xc = xf - mu
    var = jnp.mean(xc * xc, -1, keepdims=True)
    return xc * lax.rsqrt(var + NORM_EPS) * g + b


def _rope(x, cos, sin):
    w = x.shape[1]
    reps = w // LANES
    if reps > 1:
        cos = jnp.concatenate([cos] * reps, axis=1)
        sin = jnp.concatenate([sin] * reps, axis=1)
    lane = lax.broadcasted_iota(jnp.int32, x.shape, 1)
    first_half = (lane % HEAD_DIM) < (HEAD_DIM // 2)
    partner = jnp.where(first_half, pltpu.roll(x, w - HEAD_DIM // 2, 1), pltpu.roll(x, HEAD_DIM // 2, 1))
    return x * cos + partner * sin


def _dot(a, b):
    return jnp.dot(a.astype(BF16), b.astype(BF16), preferred_element_type=F32)


def _dot_nt(a, b):
    return lax.dot_general(a.astype(BF16), b.astype(BF16), (((1,), (1,)), ((), ())), preferred_element_type=F32)


def _dot_tn(a, b):
    return lax.dot_general(a.astype(BF16), b.astype(BF16), (((0,), (0,)), ((), ())), preferred_element_type=F32)


N_PROMPT_TILES = N_PROMPT // TM


def _inproj_kernel(x_ref, w_ref, op_ref, od_ref):
    i = pl.program_id(0)

    @pl.when(i < N_PROMPT_TILES)
    def _():
        op_ref[...] = jnp.dot(x_ref[...].astype(BF16), w_ref[...], preferred_element_type=F32)

    @pl.when(i == N_PROMPT_TILES)
    def _():
        od_ref[...] = jnp.dot(x_ref[0:DEC_BATCH, :].astype(BF16), w_ref[...], preferred_element_type=F32)


def _inproj(x, w):
    return pl.pallas_call(
        _inproj_kernel,
        grid=(N_TILES,),
        in_specs=[pl.BlockSpec((TM, D_MODEL), lambda i: (i, 0)),
                  pl.BlockSpec((D_MODEL, D_PROJ), lambda i: (0, 0))],
        out_specs=[pl.BlockSpec((TM, D_PROJ), lambda i: (jnp.minimum(i, N_PROMPT_TILES - 1), 0)),
                   pl.BlockSpec((DEC_BATCH, D_PROJ), lambda i: (0, 0))],
        out_shape=[jax.ShapeDtypeStruct((N_PROMPT, D_PROJ), F32), jax.ShapeDtypeStruct((DEC_BATCH, D_PROJ), F32)],
        compiler_params=pltpu.CompilerParams(dimension_semantics=("arbitrary",), vmem_limit_bytes=VMEM_LIMIT),
        name="inproj",
    )(x, w)


def _ssd_gate_norm(y, z, norm_w):
    y = y * _silu(z)
    return y * lax.rsqrt(jnp.mean(y * y, -1, keepdims=True) + NORM_EPS) * norm_w


def _attention_block(b, sinks_ref, q, k, v, bias, kbd, vbd, mix_ref):
    lo = lax.broadcasted_iota(jnp.int32, (BLK, LANES), 1) < HEAD_DIM
    k_sw = pltpu.roll(k, HEAD_DIM, 1)
    v_sw = pltpu.roll(v, HEAD_DIM, 1)
    for g in range(N_KV_HEADS):
        for ref, x, x_sw in ((kbd, k, k_sw), (vbd, v, v_sw)):
            a_src, b_src = (x, x_sw) if g == 0 else (x_sw, x)
            ref[b, g, 0:BLK, :] = ref[b, g, BLK:2 * BLK, :]
            ref[b, g, 2 * BLK:3 * BLK, :] = ref[b, g, 3 * BLK:4 * BLK, :]
            ref[b, g, BLK:2 * BLK, :] = jnp.where(lo, a_src, 0.0).astype(BF16)
            ref[b, g, 3 * BLK:4 * BLK, :] = jnp.where(lo, 0.0, b_src).astype(BF16)
    for m in range(N_HEADS // 2):
        g = (2 * m) // Q_PER_KV
        s = _dot_nt(q[:, m * LANES:(m + 1) * LANES], kbd[b, g]) + bias
        probs, inv = [], []
        for hh in range(2):
            sh = s[:, hh * 2 * BLK:(hh + 1) * 2 * BLK]
            sink = sinks_ref[2 * m + hh]
            mx = jnp.maximum(jnp.max(sh, -1, keepdims=True), sink)
            p = jnp.exp(sh - mx)
            inv.append(1.0 / (jnp.sum(p, -1, keepdims=True) + jnp.exp(sink - mx)))
            probs.append(p.astype(BF16))
        o = jnp.dot(jnp.concatenate(probs, axis=1), vbd[b, g], preferred_element_type=F32)
        mix_ref[b, :, m * LANES:(m + 1) * LANES] = (o * jnp.where(lo, inv[0], inv[1])).astype(mix_ref.dtype)


def _mixer_prompt_kernel(sinks_ref, q_ref, k_ref, v_ref, z_ref, xbc_ref, gb_ref, gc_ref, u_ref, dt_ref,
                         cos_ref, sin_ref, bias_ref, cw_ref, cb_ref, dtb_ref, alog_ref, dskip_ref, nw_ref, scw_ref,
                         mix_ref, klast_ref, hlast_ref, culast_ref,
                         kbd, vbd, conv_ext, cu_ext, hstate):
    j = pl.program_id(0)

    @pl.when(j == 0)
    def _():
        kbd[...] = jnp.zeros_like(kbd)
        vbd[...] = jnp.zeros_like(vbd)
        conv_ext[:, 0:8, :] = jnp.zeros((BATCH, 8, D_XBC), F32)
        cu_ext[:, 0:8, :] = jnp.zeros((BATCH, 8, D_CONV), F32)
        hstate[...] = jnp.zeros_like(hstate)

    cos = cos_ref[...]
    sin = sin_ref[...]
    bias = bias_ref[jnp.minimum(j, 1)]
    for b in range(BATCH):
        q = _rope(q_ref[b], cos, sin) * (1.0 / math.sqrt(HEAD_DIM))
        k = _rope(k_ref[b], cos, sin)
        klast_ref[b] = k
        _attention_block(b, sinks_ref, q, k, v_ref[b], bias, kbd, vbd, mix_ref)
        _ssd_block(b, z_ref, xbc_ref, dt_ref, cw_ref, cb_ref, dtb_ref, alog_ref, dskip_ref, nw_ref,
                   mix_ref, hlast_ref, conv_ext, hstate)
        _conv_block(b, gb_ref, gc_ref, u_ref, scw_ref, mix_ref, culast_ref, cu_ext)


def _ssd_block(b, z_ref, xbc_ref, dt_ref, cw_ref, cb_ref, dtb_ref, alog_ref, dskip_ref, nw_ref,
               mix_ref, hlast_ref, conv_ext, hstate):
    conv_ext[b, 8:8 + BLK, :] = xbc_ref[b]
    acc = conv_ext[b, 5:5 + BLK, :] * cw_ref[0:1, :]
    for t in range(1, SSD_CONV):
        acc = acc + conv_ext[b, 5 + t:5 + t + BLK, :] * cw_ref[t:t + 1, :]
    conv_ext[b, 0:8, :] = conv_ext[b, BLK:BLK + 8, :]
    xbc = _silu(acc + cb_ref[...])
    xs = xbc[:, 0:D_SSD]
    bm = xbc[:, D_SSD:D_SSD + LANES]
    cm = xbc[:, D_SSD + LANES:D_SSD + 2 * LANES]
    dt = _softplus(dt_ref[b] + dtb_ref[...])
    dta = dt * (-jnp.exp(alog_ref[...]))
    ti = lax.broadcasted_iota(jnp.int32, (BLK, BLK), 0)
    si = lax.broadcasted_iota(jnp.int32, (BLK, BLK), 1)
    causal = si <= ti
    cum = jnp.dot(causal.astype(F32), dta, preferred_element_type=F32, precision=lax.Precision.HIGHEST)
    cum_t = cum.T
    ys = []
    for g in range(SSD_GROUPS):
        sl = slice(g * SSD_STATE, (g + 1) * SSD_STATE)
        b_g = bm[:, sl]
        c_g = cm[:, sl]
        cb_scores = _dot_nt(c_g, b_g)
        for hh in range(SSD_HEADS // SSD_GROUPS):
            h = g * (SSD_HEADS // SSD_GROUPS) + hh
            ccol = cum[:, h:h + 1]
            crow = cum_t[h:h + 1, :]
            clast = cum[BLK - 1:BLK, h:h + 1]
            decay = jnp.exp(jnp.where(causal, ccol - crow, -jnp.inf))
            x_h = xs[:, h * SSD_HEAD_DIM:(h + 1) * SSD_HEAD_DIM]
            xdt = x_h * dt[:, h:h + 1]
            h_prev = hstate[b, h]
            y = _dot(cb_scores * decay, xdt)
            y = y + _dot_nt(c_g, h_prev) * jnp.exp(ccol)
            states = _dot_tn(xdt * jnp.exp(clast - ccol), b_g)
            hstate[b, h] = h_prev * jnp.exp(clast) + states
            ys.append(y)
    y = jnp.concatenate(ys, axis=1) + dskip_ref[...] * xs
    mix_ref[b, :, D_ATTN:D_ATTN + D_SSD] = _ssd_gate_norm(y, z_ref[b], nw_ref[...]).astype(mix_ref.dtype)
    hlast_ref[b] = hstate[b]


def _conv_block(b, gb_ref, gc_ref, u_ref, scw_ref, mix_ref, culast_ref, cu_ext):
    cu_ext[b, 8:8 + BLK, :] = gc_ref[b] * u_ref[b]
    cc = cu_ext[b, 6:6 + BLK, :] * scw_ref[0:1, :]
    for t in range(1, CONV_WIDTH):
        cc = cc + cu_ext[b, 6 + t:6 + t + BLK, :] * scw_ref[t:t + 1, :]
    tail = cu_ext[b, BLK:BLK + 8, :]
    cu_ext[b, 0:8, :] = tail
    culast_ref[b] = tail
    mix_ref[b, :, D_ATTN + D_SSD:D_MIX] = (gb_ref[b] * cc).astype(mix_ref.dtype)


def _attn_bias():
    r = jnp.arange(BLK, dtype=jnp.int32)[:, None]
    c = jnp.arange(4 * BLK, dtype=jnp.int32)[None, :] % (2 * BLK)
    window = (c >= r) & (c - BLK <= r)
    first = window & (c >= BLK)
    return jnp.where(jnp.stack([first, window]), 0.0, -jnp.inf).astype(F32)


def _mixer_prompt(proj, sinks, cos_tab, sin_tab, cw, cb, dtb, alog, dskip, nw, scw):
    def pblk(width, col):
        return pl.BlockSpec((BATCH, BLK, width), lambda j, *_: (0, j, col // width))

    def full(a):
        return pl.BlockSpec(a.shape, lambda j, *_: (0,) * a.ndim)

    bias = _attn_bias()
    params = (cw, cb, dtb, alog, dskip, nw, scw)
    out_shape = [jax.ShapeDtypeStruct((BATCH, SEQ, D_MIX), BF16),
                 jax.ShapeDtypeStruct((BATCH, BLK, D_KV), F32),
                 jax.ShapeDtypeStruct((BATCH, SSD_HEADS, SSD_HEAD_DIM, SSD_STATE), F32),
                 jax.ShapeDtypeStruct((BATCH, 8, D_CONV), F32)]
    grid_spec = pltpu.PrefetchScalarGridSpec(
        num_scalar_prefetch=1,
        grid=(N_BLK,),
        in_specs=[pblk(D_ATTN, C_Q), pblk(D_KV, C_K), pblk(D_KV, C_V), pblk(D_SSD, C_Z), pblk(D_XBC, C_XBC),
                  pblk(D_CONV, C_GB), pblk(D_CONV, C_GC), pblk(D_CONV, C_U), pblk(LANES, C_DT),
                  pl.BlockSpec((BLK, LANES), lambda j, *_: (j, 0)),
                  pl.BlockSpec((BLK, LANES), lambda j, *_: (j, 0)), full(bias)] + [full(a) for a in params],
        out_specs=[pl.BlockSpec((BATCH, BLK, D_MIX), lambda j, *_: (0, j, 0))] + [full(o) for o in out_shape[1:]],
        scratch_shapes=[pltpu.VMEM((BATCH, N_KV_HEADS, 4 * BLK, LANES), BF16),
                        pltpu.VMEM((BATCH, N_KV_HEADS, 4 * BLK, LANES), BF16),
                        pltpu.VMEM((BATCH, BLK + 8, D_XBC), F32), pltpu.VMEM((BATCH, BLK + 8, D_CONV), F32),
                        pltpu.VMEM((BATCH, SSD_HEADS, SSD_HEAD_DIM, SSD_STATE), F32)],
    )
    return pl.pallas_call(
        _mixer_prompt_kernel,
        grid_spec=grid_spec,
        out_shape=out_shape,
        compiler_params=pltpu.CompilerParams(dimension_semantics=("arbitrary",), vmem_limit_bytes=VMEM_LIMIT),
        name="mixer_prompt",
    )(sinks, *([proj] * 9), cos_tab, sin_tab, bias, *params)


DEC_BB = 32


def _dec_attn_kernel(q_ref, k_ref, v_ref, kc_ref, vc_ref, cos_ref, sin_ref, sink_ref,
                     o_ref, knew_ref, qbd, kn8, vn8, o_scr):
    cos = cos_ref[...]
    sin = sin_ref[...]
    q = _rope(q_ref[...], cos, sin) * (1.0 / math.sqrt(HEAD_DIM))
    kn = _rope(k_ref[...], cos, sin)
    vn = v_ref[...]
    knew_ref[...] = kn
    lane = lax.broadcasted_iota(jnp.int32, (DEC_BB, LANES), 1)
    for h in range(N_HEADS):
        g = h // Q_PER_KV
        pair = q[:, (h // 2) * LANES:(h // 2 + 1) * LANES]
        if h % 2 != g:
            pair = pltpu.roll(pair, HEAD_DIM, 1)
        qbd[pl.ds(h, DEC_BB, stride=8), :] = jnp.where(lane // HEAD_DIM == g, pair, 0.0)
        kn8[pl.ds(h, DEC_BB, stride=8), :] = kn
        vn8[pl.ds(h, DEC_BB, stride=8), :] = vn
    q3 = qbd[...].reshape(DEC_BB, N_HEADS, LANES)
    kn3 = kn8[...].reshape(DEC_BB, N_HEADS, LANES)
    vn3 = vn8[...].reshape(DEC_BB, N_HEADS, LANES)
    s3 = jnp.einsum('bhl,bsl->bhs', q3.astype(BF16), kc_ref[...].astype(BF16), preferred_element_type=F32)
    s_self = jnp.sum(q3 * kn3, axis=-1, keepdims=True)
    sink3 = sink_ref[...][None, :, 0:1]
    m = jnp.maximum(jnp.maximum(jnp.max(s3, -1, keepdims=True), s_self), sink3)
    p3 = jnp.exp(s3 - m)
    p_self = jnp.exp(s_self - m)
    denom = jnp.sum(p3, -1, keepdims=True) + p_self + jnp.exp(sink3 - m)
    o3 = jnp.einsum('bhs,bsl->bhl', p3.astype(BF16), vc_ref[...].astype(BF16), preferred_element_type=F32)
    o3 = (o3 + p_self * vn3) * (1.0 / denom)
    h3 = lax.broadcasted_iota(jnp.int32, o3.shape, 1)
    l3 = lax.broadcasted_iota(jnp.int32, o3.shape, 2)
    o3 = jnp.where(l3 // HEAD_DIM == h3 // Q_PER_KV, o3, 0.0)
    o_scr[...] = o3.reshape(DEC_BB * N_HEADS, LANES)
    for pair in range(N_HEADS // 2):
        g = (2 * pair) // Q_PER_KV
        a = o_scr[pl.ds(2 * pair, DEC_BB, stride=8), :]
        b = o_scr[pl.ds(2 * pair + 1, DEC_BB, stride=8), :]
        if g == 1:
            a = pltpu.roll(a, HEAD_DIM, 1)
        else:
            b = pltpu.roll(b, HEAD_DIM, 1)
        o_ref[:, pair * LANES:(pair + 1) * LANES] = a + b


def _dec_attn(proj, kc, vc, cos1, sin1, sink_rows):
    def pblk(width, col):
        return pl.BlockSpec((DEC_BB, width), lambda i: (i, col // width))

    def full(a):
        return pl.BlockSpec(a.shape, lambda i: (0,) * a.ndim)

    return pl.pallas_call(
        _dec_attn_kernel,
        grid=(DEC_BATCH // DEC_BB,),
        in_specs=[pblk(D_ATTN, C_Q), pblk(D_KV, C_K), pblk(D_KV, C_V),
                  pl.BlockSpec((DEC_BB, WINDOW, D_KV), lambda i: (i, 0, 0)),
                  pl.BlockSpec((DEC_BB, WINDOW, D_KV), lambda i: (i, 0, 0)),
                  full(cos1), full(sin1), full(sink_rows)],
        out_specs=[pl.BlockSpec((DEC_BB, D_ATTN), lambda i: (i, 0)),
                   pl.BlockSpec((DEC_BB, D_KV), lambda i: (i, 0))],
        out_shape=[jax.ShapeDtypeStruct((DEC_BATCH, D_ATTN), F32),
                   jax.ShapeDtypeStruct((DEC_BATCH, D_KV), F32)],
        scratch_shapes=[pltpu.VMEM((DEC_BB * N_HEADS, LANES), F32)] * 4,
        compiler_params=pltpu.CompilerParams(dimension_semantics=("arbitrary",), vmem_limit_bytes=VMEM_LIMIT),
        name="dec_attn",
    )(proj, proj, proj, kc, vc, cos1, sin1, sink_rows)


D_STATE_FLAT = SSD_HEAD_DIM * SSD_STATE


def _dec_ssd_kernel(oattn_ref, z_ref, xbc_ref, gb_ref, gc_ref, u_ref, dt_ref, sbuf_ref, cbuf_ref, st_ref,
                    cw_ref, cb_ref, dtb_ref, alog_ref, dskip_ref, nw_ref, scw_ref,
                    mix_ref, stnew_ref, sbufnew_ref, cbufnew_ref,
                    st_t, stnew_t, xdt_t, b_t, c_t, dec_t, y_t):
    xbc_new = xbc_ref[...]
    acc = sbuf_ref[:, 0:D_XBC] * cw_ref[0:1, :]
    for t in range(1, SSD_CONV - 1):
        acc = acc + sbuf_ref[:, t * D_XBC:(t + 1) * D_XBC] * cw_ref[t:t + 1, :]
    acc = acc + xbc_new * cw_ref[SSD_CONV - 1:SSD_CONV, :]
    sbufnew_ref[:, 0:(SSD_CONV - 2) * D_XBC] = sbuf_ref[:, D_XBC:(SSD_CONV - 1) * D_XBC]
    sbufnew_ref[:, (SSD_CONV - 2) * D_XBC:(SSD_CONV - 1) * D_XBC] = xbc_new
    xbc = _silu(acc + cb_ref[...])
    xs = xbc[:, 0:D_SSD]
    dt = _softplus(dt_ref[...] + dtb_ref[...])
    dta = dt * (-jnp.exp(alog_ref[...]))
    dec_t[...] = jnp.exp(dta).T
    dt_t = dt.T
    xs_t = xs.T
    for h in range(SSD_HEADS):
        sl = slice(h * SSD_HEAD_DIM, (h + 1) * SSD_HEAD_DIM)
        xdt_t[sl, :] = xs_t[sl, :] * dt_t[h:h + 1, :]
    b_t[...] = xbc[:, D_SSD:D_SSD + LANES].T
    c_t[...] = xbc[:, D_SSD + LANES:D_SSD + 2 * LANES].T
    for h in range(SSD_HEADS):
        g = h // (SSD_HEADS // SSD_GROUPS)
        st_t[...] = st_ref[:, h * D_STATE_FLAT:(h + 1) * D_STATE_FLAT].T
        dec = dec_t[h:h + 1, :]
        bg = b_t[g * SSD_STATE:(g + 1) * SSD_STATE, :]
        cg = c_t[g * SSD_STATE:(g + 1) * SSD_STATE, :]

        def body(p, carry, h=h, dec=dec, bg=bg, cg=cg):
            off = pl.multiple_of(p * SSD_STATE, SSD_STATE)
            new = st_t[pl.ds(off, SSD_STATE), :] * dec + xdt_t[pl.ds(h * SSD_HEAD_DIM + p, 1), :] * bg
            stnew_t[pl.ds(off, SSD_STATE), :] = new
            y_t[pl.ds(h * SSD_HEAD_DIM + p, 1), :] = jnp.sum(new * cg, axis=0, keepdims=True)
            return carry

        lax.fori_loop(0, SSD_HEAD_DIM, body, 0)
        stnew_ref[:, h * D_STATE_FLAT:(h + 1) * D_STATE_FLAT] = stnew_t[...].T
    y = y_t[...].T + dskip_ref[...] * xs
    mix_ref[:, 0:D_ATTN] = oattn_ref[...].astype(mix_ref.dtype)
    mix_ref[:, D_ATTN:D_ATTN + D_SSD] = _ssd_gate_norm(y, z_ref[...], nw_ref[...]).astype(mix_ref.dtype)

    cu = gc_ref[...] * u_ref[...]
    cc = cbuf_ref[:, 0:D_CONV] * scw_ref[0:1, :] + cbuf_ref[:, D_CONV:2 * D_CONV] * scw_ref[1:2, :]
    cc = cc + cu * scw_ref[2:3, :]
    cbufnew_ref[:, 0:D_CONV] = cbuf_ref[:, D_CONV:2 * D_CONV]
    cbufnew_ref[:, D_CONV:2 * D_CONV] = cu
    mix_ref[:, D_ATTN + D_SSD:D_MIX] = (gb_ref[...] * cc).astype(mix_ref.dtype)


def _dec_ssd(oattn, proj, sbuf, cbuf, st, cw, cb, dtb, alog, dskip, nw, scw):
    def pblk(width, col):
        return pl.BlockSpec((DEC_BATCH, width), lambda i: (0, col // width))

    def full(a):
        return pl.BlockSpec(a.shape, lambda i: (0,) * a.ndim)

    params = (cw, cb, dtb, alog, dskip, nw, scw)
    return pl.pallas_call(
        _dec_ssd_kernel,
        grid=(1,),
        in_specs=[full(oattn), pblk(D_SSD, C_Z), pblk(D_XBC, C_XBC), pblk(D_CONV, C_GB), pblk(D_CONV, C_GC),
                  pblk(D_CONV, C_U), pblk(LANES, C_DT), full(sbuf), full(cbuf), full(st)]
                 + [full(a) for a in params],
        out_specs=[pl.BlockSpec((DEC_BATCH, D_MIX), lambda i: (0, 0)), full(st), full(sbuf), full(cbuf)],
        out_shape=[jax.ShapeDtypeStruct((DEC_BATCH, D_MIX), BF16), jax.ShapeDtypeStruct(st.shape, F32),
                   jax.ShapeDtypeStruct(sbuf.shape, F32), jax.ShapeDtypeStruct(cbuf.shape, F32)],
        scratch_shapes=[pltpu.VMEM((D_STATE_FLAT, DEC_BATCH), F32), pltpu.VMEM((D_STATE_FLAT, DEC_BATCH), F32),
                        pltpu.VMEM((D_SSD, DEC_BATCH), F32), pltpu.VMEM((LANES, DEC_BATCH), F32),
                        pltpu.VMEM((LANES, DEC_BATCH), F32), pltpu.VMEM((LANES, DEC_BATCH), F32),
                        pltpu.VMEM((D_SSD, DEC_BATCH), F32)],
        compiler_params=pltpu.CompilerParams(dimension_semantics=("arbitrary",), vmem_limit_bytes=VMEM_LIMIT),
        name="dec_ssd",
    )(oattn, proj, proj, proj, proj, proj, proj, sbuf, cbuf, st, *params)


def _outproj_kernel(mixp_ref, mixd_ref, x_ref, w_ref, g_ref, b_ref, o_ref):
    i = pl.program_id(0)

    @pl.when(i < N_PROMPT_TILES)
    def _():
        m = jnp.dot(mixp_ref[...], w_ref[...], preferred_element_type=F32)
        o_ref[...] = _layernorm(ALPHA * x_ref[...] + m, g_ref[...], b_ref[...])

    @pl.when(i == N_PROMPT_TILES)
    def _():
        m = jnp.dot(mixd_ref[...], w_ref[...], preferred_element_type=F32)
        o_ref[0:DEC_BATCH, :] = _layernorm(ALPHA * x_ref[0:DEC_BATCH, :] + m, g_ref[...], b_ref[...])


def _outproj_ln(mix_p, mix_d, x, w, g, b):
    row = lambda i: (i, 0)
    const = lambda i: (0, 0)
    return pl.pallas_call(
        _outproj_kernel,
        grid=(N_TILES,),
        in_specs=[pl.BlockSpec((TM, D_MIX), lambda i: (jnp.minimum(i, N_PROMPT_TILES - 1), 0)),
                  pl.BlockSpec((DEC_BATCH, D_MIX), const), pl.BlockSpec((TM, D_MODEL), row),
                  pl.BlockSpec((D_MIX, D_MODEL), const), pl.BlockSpec((1, D_MODEL), const),
                  pl.BlockSpec((1, D_MODEL), const)],
        out_specs=pl.BlockSpec((TM, D_MODEL), row),
        out_shape=jax.ShapeDtypeStruct((N_TOK, D_MODEL), F32),
        compiler_params=pltpu.CompilerParams(dimension_semantics=("arbitrary",), vmem_limit_bytes=VMEM_LIMIT),
        name="outproj_ln",
    )(mix_p, mix_d, x, w, g, b)


def _swiglu(xb, wg, wu, wd):
    gate = jnp.dot(xb, wg, preferred_element_type=F32)
    up = jnp.dot(xb, wu, preferred_element_type=F32)
    return jnp.dot((_silu(gate) * up).astype(BF16), wd, preferred_element_type=F32)


def _ffn_kernel(x_ref, wg_ref, wu_ref, wd_ref, g_ref, b_ref, o_ref):
    x = x_ref[...]
    f = _swiglu(x.astype(BF16), wg_ref[...], wu_ref[...], wd_ref[...])
    o_ref[...] = _layernorm(ALPHA * x + f, g_ref[...], b_ref[...])


def _ffn_ln(x, wg, wu, wd, g, b):
    row = lambda i: (i, 0)
    const = lambda i: (0, 0)
    whole = pl.BlockSpec(memory_space=pltpu.VMEM)
    return pl.pallas_call(
        _ffn_kernel,
        grid=(N_TILES,),
        in_specs=[pl.BlockSpec((TM, D_MODEL), row), whole, whole, whole,
                  pl.BlockSpec((1, D_MODEL), const), pl.BlockSpec((1, D_MODEL), const)],
        out_specs=pl.BlockSpec((TM, D_MODEL), row),
        out_shape=jax.ShapeDtypeStruct((N_TOK, D_MODEL), F32),
        compiler_params=pltpu.CompilerParams(dimension_semantics=("arbitrary",), vmem_limit_bytes=VMEM_LIMIT),
        name="ffn_ln",
    )(x, wg, wu, wd, g, b)


TOP_K = 2
TMOE = 512
MOE_TILES = (TOP_K * N_TOK) // TMOE + N_EXPERTS
MOE_ROWS = MOE_TILES * TMOE


def _router_kernel(x_ref, r_ref, route_ref, gate_ref, cnt_ref, base):
    i = pl.program_id(0)

    @pl.when(i == 0)
    def _():
        base[...] = jnp.zeros_like(base)

    lane = lax.broadcasted_iota(jnp.int32, (TM, LANES), 1)
    row = lax.broadcasted_iota(jnp.int32, (TM, LANES), 0) + i * TM
    valid = row < N_TOK
    rows_ok = lax.broadcasted_iota(jnp.int32, (TM, D_MODEL), 0) + i * TM < N_TOK
    x = jnp.where(rows_ok, x_ref[...], 0.0)
    logits = jnp.dot(x, r_ref[...], preferred_element_type=F32, precision=lax.Precision.HIGHEST)
    logits = jnp.where(lane < N_EXPERTS, logits, -jnp.inf)
    v1 = jnp.max(logits, -1, keepdims=True)
    i1 = jnp.min(jnp.where(logits == v1, lane, LANES), -1, keepdims=True)
    rest = jnp.where(lane == i1, -jnp.inf, logits)
    v2 = jnp.max(rest, -1, keepdims=True)
    i2 = jnp.min(jnp.where(rest == v2, lane, LANES), -1, keepdims=True)
    e2 = jnp.exp(v2 - v1)
    g1 = 1.0 / (1.0 + e2)
    oh1 = jnp.where((lane == i1) & valid, 1.0, 0.0)
    oh2 = jnp.where((lane == i2) & valid, 1.0, 0.0)
    tr = lax.broadcasted_iota(jnp.int32, (TM, TM), 0)
    tc = lax.broadcasted_iota(jnp.int32, (TM, TM), 1)
    before = jnp.where(tc < tr, 1.0, 0.0).astype(BF16)
    cum1 = jnp.dot(before, oh1.astype(BF16), preferred_element_type=F32)
    cum2 = jnp.dot(before, oh2.astype(BF16), preferred_element_type=F32)
    tot1 = jnp.sum(oh1, axis=0, keepdims=True)
    tot2 = jnp.sum(oh2, axis=0, keepdims=True)
    b = base[...]
    rank1 = jnp.sum(oh1 * (cum1 + b), -1, keepdims=True).astype(jnp.int32)
    rank2 = jnp.sum(oh2 * (cum2 + b + tot1), -1, keepdims=True).astype(jnp.int32)
    total = b + tot1 + tot2
    base[...] = total
    route_ref[...] = jnp.where(lane == 0, i1, jnp.where(lane == 1, i2, jnp.where(lane == 2, rank1,
                                                                                 jnp.where(lane == 3, rank2, 0))))
    gate_ref[...] = jnp.where(lane == 0, g1, jnp.where(lane == 1, e2 * g1, 0.0))
    cnt_ref[...] = jnp.broadcast_to(total, cnt_ref.shape).astype(jnp.int32)


def _router(x, router):
    row = lambda i: (i, 0)
    const = lambda i: (0, 0)
    return pl.pallas_call(
        _router_kernel,
        grid=(N_TILES,),
        in_specs=[pl.BlockSpec((TM, D_MODEL), row), pl.BlockSpec((D_MODEL, LANES), const)],
        out_specs=[pl.BlockSpec((TM, LANES), row), pl.BlockSpec((TM, LANES), row), pl.BlockSpec((8, LANES), const)],
        out_shape=[jax.ShapeDtypeStruct((N_TOK, LANES), jnp.int32), jax.ShapeDtypeStruct((N_TOK, LANES), F32),
                   jax.ShapeDtypeStruct((8, LANES), jnp.int32)],
        scratch_shapes=[pltpu.VMEM((1, LANES), F32)],
        compiler_params=pltpu.CompilerParams(dimension_semantics=("arbitrary",), vmem_limit_bytes=VMEM_LIMIT),
        name="router",
    )(x, router)


def _row_copy(src, src_row, dst, dst_row, sem):
    return pltpu.make_async_copy(src.at[pl.ds(src_row, 1)], dst.at[pl.ds(dst_row, 1)], sem)


DMA_UNROLL = 8


def _dispatch_kernel(pos1_ref, pos2_ref, cnt_ref, off_ref, pad_ref, x_ref, xs_hbm, sem):
    i = pl.program_id(0)
    start = i * TM

    def issue(r, carry):
        t = start + r
        _row_copy(x_ref, r, xs_hbm, pos1_ref[t], sem).start()
        _row_copy(x_ref, r, xs_hbm, pos2_ref[t], sem).start()
        return carry

    def drain(r, carry):
        _row_copy(x_ref, 0, xs_hbm, 0, sem).wait()
        return carry

    @pl.when(i < N_PROMPT_TILES)
    def _():
        lax.fori_loop(0, TM, issue, 0, unroll=DMA_UNROLL)
        for _ in range(TOP_K):
            pltpu.make_async_copy(x_ref, xs_hbm.at[pl.ds(0, TM)], sem).wait()

    @pl.when(i == N_PROMPT_TILES)
    def _():
        lax.fori_loop(0, DEC_BATCH, issue, 0)
        lax.fori_loop(0, TOP_K * DEC_BATCH, drain, 0)
        for e in range(N_EXPERTS):
            lo = off_ref[e] + cnt_ref[e]

            def fill(q, carry, lo=lo):
                _row_copy(x_ref, 0, xs_hbm, lo + q, sem).start()
                return carry

            lax.fori_loop(0, pad_ref[e], fill, 0)
            lax.fori_loop(0, pad_ref[e], drain, 0)


def _dispatch(x, pos1, pos2, counts, offs, pads):
    return pl.pallas_call(
        _dispatch_kernel,
        grid_spec=pltpu.PrefetchScalarGridSpec(
            num_scalar_prefetch=5, grid=(N_TILES,),
            in_specs=[pl.BlockSpec((TM, D_MODEL), lambda i, *_: (i, 0))],
            out_specs=pl.BlockSpec(memory_space=pl.ANY),
            scratch_shapes=[pltpu.SemaphoreType.DMA(())]),
        out_shape=jax.ShapeDtypeStruct((MOE_ROWS, D_MODEL), F32),
        compiler_params=pltpu.CompilerParams(dimension_semantics=("arbitrary",), vmem_limit_bytes=VMEM_LIMIT),
        name="moe_dispatch",
    )(pos1, pos2, counts, offs, pads, x)


def _gffn_kernel(te_ref, nv_ref, xs_ref, wg_ref, wu_ref, wd_ref, ys_ref):
    @pl.when(pl.program_id(0) < nv_ref[0])
    def _():
        ys_ref[...] = _swiglu(xs_ref[...].astype(BF16), wg_ref[0], wu_ref[0], wd_ref[0])


def _gffn(xs, tile_expert, n_valid, wg, wu, wd):
    def tile(i, te, nv):
        return (jnp.minimum(i, nv[0] - 1), 0)

    def expert(i, te, nv):
        return (te[jnp.minimum(i, nv[0] - 1)], 0, 0)

    return pl.pallas_call(
        _gffn_kernel,
        grid_spec=pltpu.PrefetchScalarGridSpec(
            num_scalar_prefetch=2, grid=(MOE_TILES,),
            in_specs=[pl.BlockSpec((TMOE, D_MODEL), tile), pl.BlockSpec((1, D_MODEL, D_FF), expert),
                      pl.BlockSpec((1, D_MODEL, D_FF), expert), pl.BlockSpec((1, D_FF, D_MODEL), expert)],
            out_specs=pl.BlockSpec((TMOE, D_MODEL), tile)),
        out_shape=jax.ShapeDtypeStruct((MOE_ROWS, D_MODEL), F32),
        compiler_params=pltpu.CompilerParams(dimension_semantics=("arbitrary",), vmem_limit_bytes=VMEM_LIMIT),
        name="moe_ffn",
    )(tile_expert, n_valid, xs, wg, wu, wd)


def _combine_kernel(pos1_ref, pos2_ref, x_ref, gate_ref, ys_hbm, g_ref, b_ref, o_ref, rows, sem):
    start = pl.program_id(0) * TM

    def issue(r, carry):
        t = jnp.minimum(start + r, N_TOK - 1)
        pltpu.make_async_copy(ys_hbm.at[pl.ds(pos1_ref[t], 1)], rows.at[0, pl.ds(r, 1)], sem).start()
        pltpu.make_async_copy(ys_hbm.at[pl.ds(pos2_ref[t], 1)], rows.at[1, pl.ds(r, 1)], sem).start()
        return carry

    lax.fori_loop(0, TM, issue, 0, unroll=DMA_UNROLL)
    for s in range(TOP_K):
        pltpu.make_async_copy(ys_hbm.at[pl.ds(0, TM)], rows.at[s], sem).wait()
    gate = gate_ref[...]
    f = gate[:, 0:1] * rows[0] + gate[:, 1:2] * rows[1]
    o_ref[...] = _layernorm(ALPHA * x_ref[...] + f, g_ref[...], b_ref[...])


def _combine_ln(x, gates, ys, pos1, pos2, g, b):
    row = lambda i, *_: (i, 0)
    const = lambda i, *_: (0, 0)
    return pl.pallas_call(
        _combine_kernel,
        grid_spec=pltpu.PrefetchScalarGridSpec(
            num_scalar_prefetch=2, grid=(N_TILES,),
            in_specs=[pl.BlockSpec((TM, D_MODEL), row), pl.BlockSpec((TM, LANES), row),
                      pl.BlockSpec(memory_space=pl.ANY),
                      pl.BlockSpec((1, D_MODEL), const), pl.BlockSpec((1, D_MODEL), const)],
            out_specs=pl.BlockSpec((TM, D_MODEL), row),
            scratch_shapes=[pltpu.VMEM((TOP_K, TM, D_MODEL), F32), pltpu.SemaphoreType.DMA(())]),
        out_shape=jax.ShapeDtypeStruct((N_TOK, D_MODEL), F32),
        compiler_params=pltpu.CompilerParams(dimension_semantics=("arbitrary",), vmem_limit_bytes=VMEM_LIMIT),
        name="moe_combine_ln",
    )(pos1, pos2, x, gates, ys, g, b)


def _moe_ln(x, router, wg, wu, wd, g, b):
    route, gates, cnt = _router(x, router)
    counts = cnt[0, :N_EXPERTS]
    padded = (counts + TMOE - 1) // TMOE * TMOE
    ends = jnp.cumsum(padded)
    offs = ends - padded
    pos1 = jnp.take(offs, route[:, 0]) + route[:, 2]
    pos2 = jnp.take(offs, route[:, 1]) + route[:, 3]
    tile_ends = ends // TMOE
    tile_ids = jnp.arange(MOE_TILES, dtype=jnp.int32)
    tile_expert = jnp.minimum(jnp.sum(tile_ids[:, None] >= tile_ends[None, :], axis=1), N_EXPERTS - 1)
    xs = _dispatch(x, pos1, pos2, counts, offs, padded - counts)
    ys = _gffn(xs, tile_expert.astype(jnp.int32), tile_ends[N_EXPERTS - 1:], wg, wu, wd)
    return _combine_ln(x, gates, ys, pos1, pos2, g, b)


def _rope_tables(pos):
    half = HEAD_DIM // 2
    inv_freq = ROPE_THETA ** (-jnp.arange(half, dtype=F32) / half)
    ang = pos.astype(F32)[:, None] * inv_freq[None, :]
    cos = jnp.cos(ang)
    sin = jnp.sin(ang)
    return jnp.concatenate([cos] * 4, axis=1), jnp.concatenate([-sin, sin, -sin, sin], axis=1)


def _pad_lanes(a):
    return jnp.pad(a, [(0, 0)] * (a.ndim - 1) + [(0, LANES - a.shape[-1])])


def kernel(x_prompt, x_sample, cache_win_k, cache_win_v, state_ssd, state_ssd_conv, state_conv, w_in, attn_sinks,
           ssd_conv_w, ssd_conv_b, ssd_dt_bias, ssd_a_log, ssd_d, ssd_norm_w, sconv_w, w_out, ln1_g, ln1_b, ln2_g,
           ln2_b, ffn_w_gate, ffn_w_up, ffn_w_down, moe_router, moe_w_gate, moe_w_up, moe_w_down):
    x = jnp.concatenate([x_prompt.reshape(N_PROMPT, D_MODEL), x_sample.reshape(DEC_BATCH, D_MODEL)], axis=0)
    cos_p, sin_p = _rope_tables(jnp.arange(SEQ, dtype=jnp.int32))
    cos_s, sin_s = _rope_tables(PAST_LEN + jnp.arange(1, dtype=jnp.int32))
    split = C_GB + SSD_HEADS
    w_in_r = jnp.concatenate([w_in[:, :, :C_GB], w_in[:, :, split:], _pad_lanes(w_in[:, :, C_GB:split])],
                             axis=-1).astype(BF16)
    w_out_b = w_out.astype(BF16)
    dtb = _pad_lanes(ssd_dt_bias)[:, None, :]
    alog = _pad_lanes(ssd_a_log)[:, None, :]
    dskip = jnp.repeat(ssd_d, SSD_HEAD_DIM, axis=-1)[:, None, :]
    router = _pad_lanes(moe_router)
    sink_rows = jnp.broadcast_to(attn_sinks[:, :, None], (DEPTH, N_HEADS, LANES))
    kc = cache_win_k.reshape(DEPTH, DEC_BATCH, WINDOW, D_KV)
    vc = cache_win_v.reshape(DEPTH, DEC_BATCH, WINDOW, D_KV)
    st = state_ssd.reshape(DEPTH, DEC_BATCH, SSD_HEADS * D_STATE_FLAT)
    sbuf = state_ssd_conv.reshape(DEPTH, DEC_BATCH, (SSD_CONV - 1) * D_XBC)
    cbuf = state_conv.reshape(DEPTH, DEC_BATCH, (CONV_WIDTH - 1) * D_CONV)

    outs = {n: [] for n in ("pk", "pv", "ph", "psc", "pc", "sk", "sv", "sh", "ssc", "sc")}
    for i in range(DEPTH):
        proj_p, proj_d = _inproj(x, w_in_r[i])
        proj_p = proj_p.reshape(BATCH, SEQ, D_PROJ)
        params = (ssd_conv_w[i], ssd_conv_b[i][None, :], dtb[i], alog[i], dskip[i], ssd_norm_w[i][None, :],
                  sconv_w[i])
        mix_p, klast, hlast, culast = _mixer_prompt(proj_p, attn_sinks[i], cos_p, sin_p, *params)
        oattn, knew = _dec_attn(proj_d, kc[i], vc[i], cos_s, sin_s, sink_rows[i])
        mix_d, stnew, sbufnew, cbufnew = _dec_ssd(oattn, proj_d, sbuf[i], cbuf[i], st[i], *params)
        x = _outproj_ln(mix_p.reshape(N_PROMPT, D_MIX), mix_d, x, w_out_b[i], ln1_g[i][None, :], ln1_b[i][None, :])
        j = i // 2
        if i % 2 == 0:
            x = _ffn_ln(x, ffn_w_gate[j].astype(BF16), ffn_w_up[j].astype(BF16), ffn_w_down[j].astype(BF16),
                        ln2_g[i][None, :], ln2_b[i][None, :])
        else:
            x = _moe_ln(x, router[j], moe_w_gate[j].astype(BF16), moe_w_up[j].astype(BF16),
                        moe_w_down[j].astype(BF16), ln2_g[i][None, :], ln2_b[i][None, :])
        def tail(n_rows, col, width):
            return proj_p[:, SEQ - n_rows:, col:col + width]

        outs["pk"].append(klast.reshape(BATCH, WINDOW, N_KV_HEADS, HEAD_DIM))
        outs["pv"].append(tail(WINDOW, C_V, D_KV).reshape(BATCH, WINDOW, N_KV_HEADS, HEAD_DIM))
        outs["ph"].append(hlast)
        outs["psc"].append(tail(SSD_CONV - 1, C_XBC, D_XBC))
        outs["pc"].append(culast[:, 8 - (CONV_WIDTH - 1):, :])
        vnew = proj_d[:, C_V:C_V + D_KV]
        outs["sk"].append(jnp.concatenate([kc[i][:, 1:], knew[:, None, :]], axis=1)
                          .reshape(DEC_BATCH, WINDOW, N_KV_HEADS, HEAD_DIM))
        outs["sv"].append(jnp.concatenate([vc[i][:, 1:], vnew[:, None, :]], axis=1)
                          .reshape(DEC_BATCH, WINDOW, N_KV_HEADS, HEAD_DIM))
        outs["sh"].append(stnew.reshape(DEC_BATCH, SSD_HEADS, SSD_HEAD_DIM, SSD_STATE))
        outs["ssc"].append(sbufnew.reshape(DEC_BATCH, SSD_CONV - 1, D_XBC))
        outs["sc"].append(cbufnew.reshape(DEC_BATCH, CONV_WIDTH - 1, D_CONV))
    y_prompt = x[:N_PROMPT].reshape(BATCH, SEQ, D_MODEL)
    y_sample = x[N_PROMPT:].reshape(DEC_BATCH, 1, D_MODEL)
    stk = lambda n: jnp.stack(outs[n])
    return (y_prompt, y_sample, stk("pk"), stk("pv"), stk("ph"), stk("psc"), stk("pc"),
            stk("sk"), stk("sv"), stk("sh"), stk("ssc"), stk("sc"))
```

```python
import functools
import math

import jax
import jax.numpy as jnp
from jax import lax
from jax.experimental import pallas as pl
from jax.experimental.pallas import tpu as pltpu

D_MODEL = 1024
BATCH = 2
SEQ = 8192
DEPTH = 4
DEC_BATCH = 128
PAST_LEN = 8192
N_HEADS = 8
N_KV_HEADS = 2
HEAD_DIM = 64
Q_PER_KV = N_HEADS // N_KV_HEADS
D_ATTN = N_HEADS * HEAD_DIM
D_KV = N_KV_HEADS * HEAD_DIM
WINDOW = 128
BLK = 128
ROPE_THETA = 10000.0
SSD_HEADS = 4
SSD_HEAD_DIM = 64
D_SSD = SSD_HEADS * SSD_HEAD_DIM
SSD_GROUPS = 2
SSD_STATE = 64
SSD_CONV = 4
D_XBC = D_SSD + 2 * SSD_GROUPS * SSD_STATE
D_CONV = 256
CONV_WIDTH = 3
D_MIX = D_ATTN + D_SSD + D_CONV
D_FF = 2816
N_EXPERTS = 8
ALPHA = (2 * DEPTH) ** 0.25
NORM_EPS = 1e-5

LANES = 128
N_PROMPT = BATCH * SEQ
N_TOK = N_PROMPT + DEC_BATCH
N_BLK = SEQ // BLK
TM = 512
N_TILES = pl.cdiv(N_TOK, TM)
VMEM_LIMIT = 56 * 1024 * 1024

C_Q, C_K, C_V, C_Z, C_XBC = 0, 512, 640, 768, 1024
C_GB, C_GC, C_U, C_DT = 1536, 1792, 2048, 2304
D_PROJ = C_DT + LANES

BF16 = jnp.bfloat16
F32 = jnp.float32


def _sigmoid(x):
    return 1.0 / (1.0 + jnp.exp(-x))


def _silu(x):
    return x * _sigmoid(x)


def _softplus(x):
    return jnp.maximum(x, 0.0) + jnp.log1p(jnp.exp(-jnp.abs(x)))


def _layernorm(xf, g, b):
    mu = jnp.mean(xf, -1, keepdims=True)
    xc = xf - mu
    var = jnp.mean(xc * xc, -1, keepdims=True)
    return xc * lax.rsqrt(var + NORM_EPS) * g + b


def _rope(x, cos, sin):
    w = x.shape[1]
    reps = w // LANES
    if reps > 1:
        cos = jnp.concatenate([cos] * reps, axis=1)
        sin = jnp.concatenate([sin] * reps, axis=1)
    lane = lax.broadcasted_iota(jnp.int32, x.shape, 1)
    first_half = (lane % HEAD_DIM) < (HEAD_DIM // 2)
    partner = jnp.where(first_half, pltpu.roll(x, w - HEAD_DIM // 2, 1), pltpu.roll(x, HEAD_DIM // 2, 1))
    return x * cos + partner * sin


def _dot(a, b):
    return jnp.dot(a.astype(BF16), b.astype(BF16), preferred_element_type=F32)


def _dot_nt(a, b):
    return lax.dot_general(a.astype(BF16), b.astype(BF16), (((1,), (1,)), ((), ())), preferred_element_type=F32)


def _dot_tn(a, b):
    return lax.dot_general(a.astype(BF16), b.astype(BF16), (((0,), (0,)), ((), ())), preferred_element_type=F32)


N_PROMPT_TILES = N_PROMPT // TM
TOKEN_PARAMS = pltpu.CompilerParams(dimension_semantics=("arbitrary",), vmem_limit_bytes=VMEM_LIMIT)
WHOLE_VMEM = pl.BlockSpec(memory_space=pltpu.VMEM)


def _p_spec(width):
    return pl.BlockSpec((TM, width), lambda i, *_: (jnp.minimum(i, N_PROMPT_TILES - 1), 0))


def _d_spec(width):
    return pl.BlockSpec((DEC_BATCH, width), lambda i, *_: (0, 0))


def _tile_spec(width):
    return pl.BlockSpec((TM, width), lambda i, *_: (i, 0))


def _const_spec(a):
    return pl.BlockSpec(a.shape, lambda i, *_: (0,) * a.ndim)


def _on_token_tile(fn, prompt_refs, decode_refs):
    i = pl.program_id(0)

    @pl.when(i < N_PROMPT_TILES)
    def _():
        fn(*prompt_refs)

    @pl.when(i == N_PROMPT_TILES)
    def _():
        fn(*decode_refs)


def _cast_spec(rows, cols, steps):
    return pl.BlockSpec((rows // steps, cols), lambda i, *_: (jnp.minimum(i, steps - 1), 0))


FFN_CAST_STEPS = 16


def _inproj_kernel(xp_ref, xd_ref, w_ref, *refs):
    if len(refs) == 2:
        pp_ref, pd_ref = refs
    else:
        wg_ref, wu_ref, wd_ref, pp_ref, pd_ref, wgb_ref, wub_ref, wdb_ref = refs

    def project(x_ref, o_ref):
        o_ref[...] = jnp.dot(x_ref[...].astype(BF16), w_ref[...], preferred_element_type=F32)

    _on_token_tile(project, (xp_ref, pp_ref), (xd_ref, pd_ref))
    if len(refs) > 2:
        @pl.when(pl.program_id(0) < FFN_CAST_STEPS)
        def _():
            wgb_ref[...] = wg_ref[...].astype(BF16)
            wub_ref[...] = wu_ref[...].astype(BF16)
            wdb_ref[...] = wd_ref[...].astype(BF16)


def _inproj(x_p, x_d, w, ffn_w=None):
    in_specs = [_p_spec(D_MODEL), _d_spec(D_MODEL), _const_spec(w)]
    out_specs = [_p_spec(D_PROJ), _d_spec(D_PROJ)]
    out_shape = [jax.ShapeDtypeStruct((N_PROMPT, D_PROJ), F32), jax.ShapeDtypeStruct((DEC_BATCH, D_PROJ), F32)]
    args = [x_p, x_d, w]
    if ffn_w is not None:
        cast_specs = [_cast_spec(*a.shape, FFN_CAST_STEPS) for a in ffn_w]
        in_specs += cast_specs
        out_specs += cast_specs
        out_shape += [jax.ShapeDtypeStruct(a.shape, BF16) for a in ffn_w]
        args += list(ffn_w)
    return pl.pallas_call(
        _inproj_kernel, grid=(N_TILES,), in_specs=in_specs, out_specs=out_specs, out_shape=out_shape,
        compiler_params=TOKEN_PARAMS, name="inproj",
    )(*args)


def _ssd_gate_norm(y, z, norm_w):
    y = y * _silu(z)
    return y * lax.rsqrt(jnp.mean(y * y, -1, keepdims=True) + NORM_EPS) * norm_w


def _attention_block(b, sinks_ref, q, k, v, bias, kbd, vbd, mix_ref):
    lo = lax.broadcasted_iota(jnp.int32, (BLK, LANES), 1) < HEAD_DIM
    k_sw = pltpu.roll(k, HEAD_DIM, 1)
    v_sw = pltpu.roll(v, HEAD_DIM, 1)
    for g in range(N_KV_HEADS):
        for ref, x, x_sw in ((kbd, k, k_sw), (vbd, v, v_sw)):
            a_src, b_src = (x, x_sw) if g == 0 else (x_sw, x)
            ref[b, g, 0:BLK, :] = ref[b, g, BLK:2 * BLK, :]
            ref[b, g, 2 * BLK:3 * BLK, :] = ref[b, g, 3 * BLK:4 * BLK, :]
            ref[b, g, BLK:2 * BLK, :] = jnp.where(lo, a_src, 0.0).astype(BF16)
            ref[b, g, 3 * BLK:4 * BLK, :] = jnp.where(lo, 0.0, b_src).astype(BF16)
    for m in range(N_HEADS // 2):
        g = (2 * m) // Q_PER_KV
        s = _dot_nt(q[:, m * LANES:(m + 1) * LANES], kbd[b, g]) + bias
        probs, inv = [], []
        for hh in range(2):
            sh = s[:, hh * 2 * BLK:(hh + 1) * 2 * BLK]
            sink = sinks_ref[2 * m + hh]
            mx = jnp.maximum(jnp.max(sh, -1, keepdims=True), sink)
            p = jnp.exp(sh - mx)
            inv.append(1.0 / (jnp.sum(p, -1, keepdims=True) + jnp.exp(sink - mx)))
            probs.append(p.astype(BF16))
        o = jnp.dot(jnp.concatenate(probs, axis=1), vbd[b, g], preferred_element_type=F32)
        mix_ref[b, :, m * LANES:(m + 1) * LANES] = (o * jnp.where(lo, inv[0], inv[1])).astype(mix_ref.dtype)


def _mixer_prompt_kernel(sinks_ref, q_ref, k_ref, v_ref, z_ref, xbc_ref, gb_ref, gc_ref, u_ref, dt_ref,
                         cos_ref, sin_ref, bias_ref, cw_ref, cb_ref, dtb_ref, alog_ref, dskip_ref, nw_ref, scw_ref,
                         mix_ref, klast_ref, hlast_ref, culast_ref,
                         kbd, vbd, conv_ext, cu_ext, hstate):
    j = pl.program_id(0)

    @pl.when(j == 0)
    def _():
        kbd[...] = jnp.zeros_like(kbd)
        vbd[...] = jnp.zeros_like(vbd)
        conv_ext[:, 0:8, :] = jnp.zeros((BATCH, 8, D_XBC), F32)
        cu_ext[:, 0:8, :] = jnp.zeros((BATCH, 8, D_CONV), F32)
        hstate[...] = jnp.zeros_like(hstate)

    cos = cos_ref[...]
    sin = sin_ref[...]
    bias = bias_ref[jnp.minimum(j, 1)]
    for b in range(BATCH):
        q = _rope(q_ref[b], cos, sin) * (1.0 / math.sqrt(HEAD_DIM))
        k = _rope(k_ref[b], cos, sin)
        klast_ref[b] = k
        _attention_block(b, sinks_ref, q, k, v_ref[b], bias, kbd, vbd, mix_ref)
        _ssd_block(b, z_ref, xbc_ref, dt_ref, cw_ref, cb_ref, dtb_ref, alog_ref, dskip_ref, nw_ref,
                   mix_ref, hlast_ref, conv_ext, hstate)
        _conv_block(b, gb_ref, gc_ref, u_ref, scw_ref, mix_ref, culast_ref, cu_ext)


def _ssd_block(b, z_ref, xbc_ref, dt_ref, cw_ref, cb_ref, dtb_ref, alog_ref, dskip_ref, nw_ref,
               mix_ref, hlast_ref, conv_ext, hstate):
    conv_ext[b, 8:8 + BLK, :] = xbc_ref[b]
    acc = conv_ext[b, 5:5 + BLK, :] * cw_ref[0:1, :]
    for t in range(1, SSD_CONV):
        acc = acc + conv_ext[b, 5 + t:5 + t + BLK, :] * cw_ref[t:t + 1, :]
    conv_ext[b, 0:8, :] = conv_ext[b, BLK:BLK + 8, :]
    xbc = _silu(acc + cb_ref[...])
    xs = xbc[:, 0:D_SSD]
    bm = xbc[:, D_SSD:D_SSD + LANES]
    cm = xbc[:, D_SSD + LANES:D_SSD + 2 * LANES]
    dt = _softplus(dt_ref[b] + dtb_ref[...])
    dta = dt * (-jnp.exp(alog_ref[...]))
    ti = lax.broadcasted_iota(jnp.int32, (BLK, BLK), 0)
    si = lax.broadcasted_iota(jnp.int32, (BLK, BLK), 1)
    causal = si <= ti
    cum = jnp.dot(causal.astype(F32), dta, preferred_element_type=F32, precision=lax.Precision.HIGHEST)
    cum_t = cum.T
    ys = []
    for g in range(SSD_GROUPS):
        sl = slice(g * SSD_STATE, (g + 1) * SSD_STATE)
        b_g = bm[:, sl]
        c_g = cm[:, sl]
        cb_scores = _dot_nt(c_g, b_g)
        for hh in range(SSD_HEADS // SSD_GROUPS):
            h = g * (SSD_HEADS // SSD_GROUPS) + hh
            ccol = cum[:, h:h + 1]
            crow = cum_t[h:h + 1, :]
            clast = cum[BLK - 1:BLK, h:h + 1]
            decay = jnp.exp(jnp.where(causal, ccol - crow, -jnp.inf))
            x_h = xs[:, h * SSD_HEAD_DIM:(h + 1) * SSD_HEAD_DIM]
            xdt = x_h * dt[:, h:h + 1]
            h_prev = hstate[b, h]
            y = _dot(cb_scores * decay, xdt)
            y = y + _dot_nt(c_g, h_prev) * jnp.exp(ccol)
            states = _dot_tn(xdt * jnp.exp(clast - ccol), b_g)
            hstate[b, h] = h_prev * jnp.exp(clast) + states
            ys.append(y)
    y = jnp.concatenate(ys, axis=1) + dskip_ref[...] * xs
    mix_ref[b, :, D_ATTN:D_ATTN + D_SSD] = _ssd_gate_norm(y, z_ref[b], nw_ref[...]).astype(mix_ref.dtype)
    hlast_ref[b] = hstate[b]


def _conv_block(b, gb_ref, gc_ref, u_ref, scw_ref, mix_ref, culast_ref, cu_ext):
    cu_ext[b, 8:8 + BLK, :] = gc_ref[b] * u_ref[b]
    cc = cu_ext[b, 6:6 + BLK, :] * scw_ref[0:1, :]
    for t in range(1, CONV_WIDTH):
        cc = cc + cu_ext[b, 6 + t:6 + t + BLK, :] * scw_ref[t:t + 1, :]
    tail = cu_ext[b, BLK:BLK + 8, :]
    cu_ext[b, 0:8, :] = tail
    culast_ref[b] = tail
    mix_ref[b, :, D_ATTN + D_SSD:D_MIX] = (gb_ref[b] * cc).astype(mix_ref.dtype)


def _attn_bias():
    r = jnp.arange(BLK, dtype=jnp.int32)[:, None]
    c = jnp.arange(4 * BLK, dtype=jnp.int32)[None, :] % (2 * BLK)
    window = (c >= r) & (c - BLK <= r)
    first = window & (c >= BLK)
    return jnp.where(jnp.stack([first, window]), 0.0, -jnp.inf).astype(F32)


def _mixer_prompt(proj, sinks, cos_tab, sin_tab, cw, cb, dtb, alog, dskip, nw, scw):
    def pblk(width, col):
        return pl.BlockSpec((BATCH, BLK, width), lambda j, *_: (0, j, col // width))

    def full(a):
        return pl.BlockSpec(a.shape, lambda j, *_: (0,) * a.ndim)

    bias = _attn_bias()
    params = (cw, cb, dtb, alog, dskip, nw, scw)
    out_shape = [jax.ShapeDtypeStruct((BATCH, SEQ, D_MIX), BF16),
                 jax.ShapeDtypeStruct((BATCH, BLK, D_KV), F32),
                 jax.ShapeDtypeStruct((BATCH, SSD_HEADS, SSD_HEAD_DIM, SSD_STATE), F32),
                 jax.ShapeDtypeStruct((BATCH, 8, D_CONV), F32)]
    grid_spec = pltpu.PrefetchScalarGridSpec(
        num_scalar_prefetch=1,
        grid=(N_BLK,),
        in_specs=[pblk(D_ATTN, C_Q), pblk(D_KV, C_K), pblk(D_KV, C_V), pblk(D_SSD, C_Z), pblk(D_XBC, C_XBC),
                  pblk(D_CONV, C_GB), pblk(D_CONV, C_GC), pblk(D_CONV, C_U), pblk(LANES, C_DT),
                  pl.BlockSpec((BLK, LANES), lambda j, *_: (j, 0)),
                  pl.BlockSpec((BLK, LANES), lambda j, *_: (j, 0)), full(bias)] + [full(a) for a in params],
        out_specs=[pl.BlockSpec((BATCH, BLK, D_MIX), lambda j, *_: (0, j, 0))] + [full(o) for o in out_shape[1:]],
        scratch_shapes=[pltpu.VMEM((BATCH, N_KV_HEADS, 4 * BLK, LANES), BF16),
                        pltpu.VMEM((BATCH, N_KV_HEADS, 4 * BLK, LANES), BF16),
                        pltpu.VMEM((BATCH, BLK + 8, D_XBC), F32), pltpu.VMEM((BATCH, BLK + 8, D_CONV), F32),
                        pltpu.VMEM((BATCH, SSD_HEADS, SSD_HEAD_DIM, SSD_STATE), F32)],
    )
    return pl.pallas_call(
        _mixer_prompt_kernel,
        grid_spec=grid_spec,
        out_shape=out_shape,
        compiler_params=pltpu.CompilerParams(dimension_semantics=("arbitrary",), vmem_limit_bytes=VMEM_LIMIT),
        name="mixer_prompt",
    )(sinks, *([proj] * 9), cos_tab, sin_tab, bias, *params)


DEC_BB = 32


def _dec_attn_kernel(q_ref, k_ref, v_ref, kc_ref, vc_ref, cos_ref, sin_ref, sink_ref,
                     o_ref, knew_ref, qbd, kn8, vn8, o_scr):
    cos = cos_ref[...]
    sin = sin_ref[...]
    q = _rope(q_ref[...], cos, sin) * (1.0 / math.sqrt(HEAD_DIM))
    kn = _rope(k_ref[...], cos, sin)
    vn = v_ref[...]
    knew_ref[...] = kn
    lane = lax.broadcasted_iota(jnp.int32, (DEC_BB, LANES), 1)
    for h in range(N_HEADS):
        g = h // Q_PER_KV
        pair = q[:, (h // 2) * LANES:(h // 2 + 1) * LANES]
        if h % 2 != g:
            pair = pltpu.roll(pair, HEAD_DIM, 1)
        qbd[pl.ds(h, DEC_BB, stride=8), :] = jnp.where(lane // HEAD_DIM == g, pair, 0.0)
        kn8[pl.ds(h, DEC_BB, stride=8), :] = kn
        vn8[pl.ds(h, DEC_BB, stride=8), :] = vn
    q3 = qbd[...].reshape(DEC_BB, N_HEADS, LANES)
    kn3 = kn8[...].reshape(DEC_BB, N_HEADS, LANES)
    vn3 = vn8[...].reshape(DEC_BB, N_HEADS, LANES)
    s3 = jnp.einsum('bhl,bsl->bhs', q3.astype(BF16), kc_ref[...].astype(BF16), preferred_element_type=F32)
    s_self = jnp.sum(q3 * kn3, axis=-1, keepdims=True)
    sink3 = sink_ref[...][None, :, 0:1]
    m = jnp.maximum(jnp.maximum(jnp.max(s3, -1, keepdims=True), s_self), sink3)
    p3 = jnp.exp(s3 - m)
    p_self = jnp.exp(s_self - m)
    denom = jnp.sum(p3, -1, keepdims=True) + p_self + jnp.exp(sink3 - m)
    o3 = jnp.einsum('bhs,bsl->bhl', p3.astype(BF16), vc_ref[...].astype(BF16), preferred_element_type=F32)
    o3 = (o3 + p_self * vn3) * (1.0 / denom)
    h3 = lax.broadcasted_iota(jnp.int32, o3.shape, 1)
    l3 = lax.broadcasted_iota(jnp.int32, o3.shape, 2)
    o3 = jnp.where(l3 // HEAD_DIM == h3 // Q_PER_KV, o3, 0.0)
    o_scr[...] = o3.reshape(DEC_BB * N_HEADS, LANES)
    for pair in range(N_HEADS // 2):
        g = (2 * pair) // Q_PER_KV
        a = o_scr[pl.ds(2 * pair, DEC_BB, stride=8), :]
        b = o_scr[pl.ds(2 * pair + 1, DEC_BB, stride=8), :]
        if g == 1:
            a = pltpu.roll(a, HEAD_DIM, 1)
        else:
            b = pltpu.roll(b, HEAD_DIM, 1)
        o_ref[:, pair * LANES:(pair + 1) * LANES] = a + b


def _dec_attn(proj, kc, vc, cos1, sin1, sink_rows):
    def pblk(width, col):
        return pl.BlockSpec((DEC_BB, width), lambda i: (i, col // width))

    def full(a):
        return pl.BlockSpec(a.shape, lambda i: (0,) * a.ndim)

    return pl.pallas_call(
        _dec_attn_kernel,
        grid=(DEC_BATCH // DEC_BB,),
        in_specs=[pblk(D_ATTN, C_Q), pblk(D_KV, C_K), pblk(D_KV, C_V),
                  pl.BlockSpec((DEC_BB, WINDOW, D_KV), lambda i: (i, 0, 0)),
                  pl.BlockSpec((DEC_BB, WINDOW, D_KV), lambda i: (i, 0, 0)),
                  full(cos1), full(sin1), full(sink_rows)],
        out_specs=[pl.BlockSpec((DEC_BB, D_ATTN), lambda i: (i, 0)),
                   pl.BlockSpec((DEC_BB, D_KV), lambda i: (i, 0))],
        out_shape=[jax.ShapeDtypeStruct((DEC_BATCH, D_ATTN), F32),
                   jax.ShapeDtypeStruct((DEC_BATCH, D_KV), F32)],
        scratch_shapes=[pltpu.VMEM((DEC_BB * N_HEADS, LANES), F32)] * 4,
        compiler_params=pltpu.CompilerParams(dimension_semantics=("arbitrary",), vmem_limit_bytes=VMEM_LIMIT),
        name="dec_attn",
    )(proj, proj, proj, kc, vc, cos1, sin1, sink_rows)


D_STATE_FLAT = SSD_HEAD_DIM * SSD_STATE


def _dec_ssd_kernel(oattn_ref, z_ref, xbc_ref, gb_ref, gc_ref, u_ref, dt_ref, sbuf_ref, cbuf_ref, st_ref,
                    cw_ref, cb_ref, dtb_ref, alog_ref, dskip_ref, nw_ref, scw_ref,
                    mix_ref, stnew_ref, sbufnew_ref, cbufnew_ref,
                    st_t, stnew_t, xdt_t, b_t, c_t, dec_t, y_t):
    xbc_new = xbc_ref[...]
    acc = sbuf_ref[:, 0:D_XBC] * cw_ref[0:1, :]
    for t in range(1, SSD_CONV - 1):
        acc = acc + sbuf_ref[:, t * D_XBC:(t + 1) * D_XBC] * cw_ref[t:t + 1, :]
    acc = acc + xbc_new * cw_ref[SSD_CONV - 1:SSD_CONV, :]
    sbufnew_ref[:, 0:(SSD_CONV - 2) * D_XBC] = sbuf_ref[:, D_XBC:(SSD_CONV - 1) * D_XBC]
    sbufnew_ref[:, (SSD_CONV - 2) * D_XBC:(SSD_CONV - 1) * D_XBC] = xbc_new
    xbc = _silu(acc + cb_ref[...])
    xs = xbc[:, 0:D_SSD]
    dt = _softplus(dt_ref[...] + dtb_ref[...])
    dta = dt * (-jnp.exp(alog_ref[...]))
    dec_t[...] = jnp.exp(dta).T
    dt_t = dt.T
    xs_t = xs.T
    for h in range(SSD_HEADS):
        sl = slice(h * SSD_HEAD_DIM, (h + 1) * SSD_HEAD_DIM)
        xdt_t[sl, :] = xs_t[sl, :] * dt_t[h:h + 1, :]
    b_t[...] = xbc[:, D_SSD:D_SSD + LANES].T
    c_t[...] = xbc[:, D_SSD + LANES:D_SSD + 2 * LANES].T
    for h in range(SSD_HEADS):
        g = h // (SSD_HEADS // SSD_GROUPS)
        st_t[...] = st_ref[:, h * D_STATE_FLAT:(h + 1) * D_STATE_FLAT].T
        dec = dec_t[h:h + 1, :]
        bg = b_t[g * SSD_STATE:(g + 1) * SSD_STATE, :]
        cg = c_t[g * SSD_STATE:(g + 1) * SSD_STATE, :]

        def body(p, carry, h=h, dec=dec, bg=bg, cg=cg):
            off = pl.multiple_of(p * SSD_STATE, SSD_STATE)
            new = st_t[pl.ds(off, SSD_STATE), :] * dec + xdt_t[pl.ds(h * SSD_HEAD_DIM + p, 1), :] * bg
            stnew_t[pl.ds(off, SSD_STATE), :] = new
            y_t[pl.ds(h * SSD_HEAD_DIM + p, 1), :] = jnp.sum(new * cg, axis=0, keepdims=True)
            return carry

        lax.fori_loop(0, SSD_HEAD_DIM, body, 0)
        stnew_ref[:, h * D_STATE_FLAT:(h + 1) * D_STATE_FLAT] = stnew_t[...].T
    y = y_t[...].T + dskip_ref[...] * xs
    mix_ref[:, 0:D_ATTN] = oattn_ref[...].astype(mix_ref.dtype)
    mix_ref[:, D_ATTN:D_ATTN + D_SSD] = _ssd_gate_norm(y, z_ref[...], nw_ref[...]).astype(mix_ref.dtype)

    cu = gc_ref[...] * u_ref[...]
    cc = cbuf_ref[:, 0:D_CONV] * scw_ref[0:1, :] + cbuf_ref[:, D_CONV:2 * D_CONV] * scw_ref[1:2, :]
    cc = cc + cu * scw_ref[2:3, :]
    cbufnew_ref[:, 0:D_CONV] = cbuf_ref[:, D_CONV:2 * D_CONV]
    cbufnew_ref[:, D_CONV:2 * D_CONV] = cu
    mix_ref[:, D_ATTN + D_SSD:D_MIX] = (gb_ref[...] * cc).astype(mix_ref.dtype)


def _dec_ssd(oattn, proj, sbuf, cbuf, st, cw, cb, dtb, alog, dskip, nw, scw):
    def pblk(width, col):
        return pl.BlockSpec((DEC_BATCH, width), lambda i: (0, col // width))

    def full(a):
        return pl.BlockSpec(a.shape, lambda i: (0,) * a.ndim)

    params = (cw, cb, dtb, alog, dskip, nw, scw)
    return pl.pallas_call(
        _dec_ssd_kernel,
        grid=(1,),
        in_specs=[full(oattn), pblk(D_SSD, C_Z), pblk(D_XBC, C_XBC), pblk(D_CONV, C_GB), pblk(D_CONV, C_GC),
                  pblk(D_CONV, C_U), pblk(LANES, C_DT), full(sbuf), full(cbuf), full(st)]
                 + [full(a) for a in params],
        out_specs=[pl.BlockSpec((DEC_BATCH, D_MIX), lambda i: (0, 0)), full(st), full(sbuf), full(cbuf)],
        out_shape=[jax.ShapeDtypeStruct((DEC_BATCH, D_MIX), BF16), jax.ShapeDtypeStruct(st.shape, F32),
                   jax.ShapeDtypeStruct(sbuf.shape, F32), jax.ShapeDtypeStruct(cbuf.shape, F32)],
        scratch_shapes=[pltpu.VMEM((D_STATE_FLAT, DEC_BATCH), F32), pltpu.VMEM((D_STATE_FLAT, DEC_BATCH), F32),
                        pltpu.VMEM((D_SSD, DEC_BATCH), F32), pltpu.VMEM((LANES, DEC_BATCH), F32),
                        pltpu.VMEM((LANES, DEC_BATCH), F32), pltpu.VMEM((LANES, DEC_BATCH), F32),
                        pltpu.VMEM((D_SSD, DEC_BATCH), F32)],
        compiler_params=pltpu.CompilerParams(dimension_semantics=("arbitrary",), vmem_limit_bytes=VMEM_LIMIT),
        name="dec_ssd",
    )(oattn, proj, proj, proj, proj, proj, proj, sbuf, cbuf, st, *params)


def _swiglu(xb, wg, wu, wd):
    gate = jnp.dot(xb, wg, preferred_element_type=F32)
    up = jnp.dot(xb, wu, preferred_element_type=F32)
    return jnp.dot((_silu(gate) * up).astype(BF16), wd, preferred_element_type=F32)


def _mixer_residual_norm(mix_ref, x_ref, wo_ref, g_ref, b_ref):
    m = jnp.dot(mix_ref[...], wo_ref[...], preferred_element_type=F32)
    return _layernorm(ALPHA * x_ref[...] + m, g_ref[...], b_ref[...])


def _dense_tail_kernel(mixp_ref, mixd_ref, xp_ref, xd_ref, wo_ref, g1_ref, b1_ref, wg_ref, wu_ref, wd_ref,
                       g2_ref, b2_ref, op_ref, od_ref):
    def tail(mix_ref, x_ref, o_ref):
        x1 = _mixer_residual_norm(mix_ref, x_ref, wo_ref, g1_ref, b1_ref)
        f = _swiglu(x1.astype(BF16), wg_ref[...], wu_ref[...], wd_ref[...])
        o_ref[...] = _layernorm(ALPHA * x1 + f, g2_ref[...], b2_ref[...])

    _on_token_tile(tail, (mixp_ref, xp_ref, op_ref), (mixd_ref, xd_ref, od_ref))


def _dense_tail(mix_p, mix_d, x_p, x_d, wo, g1, b1, wg, wu, wd, g2, b2):
    return pl.pallas_call(
        _dense_tail_kernel,
        grid=(N_TILES,),
        in_specs=[_p_spec(D_MIX), _d_spec(D_MIX), _p_spec(D_MODEL), _d_spec(D_MODEL), _const_spec(wo),
                  _const_spec(g1), _const_spec(b1), WHOLE_VMEM, WHOLE_VMEM, WHOLE_VMEM,
                  _const_spec(g2), _const_spec(b2)],
        out_specs=[_p_spec(D_MODEL), _d_spec(D_MODEL)],
        out_shape=[jax.ShapeDtypeStruct((N_PROMPT, D_MODEL), F32), jax.ShapeDtypeStruct((DEC_BATCH, D_MODEL), F32)],
        compiler_params=TOKEN_PARAMS, name="dense_tail",
    )(mix_p, mix_d, x_p, x_d, wo, g1, b1, wg, wu, wd, g2, b2)


TOP_K = 2
TMOE = 512
MOE_TILES = (TOP_K * N_TOK) // TMOE + N_EXPERTS
MOE_ROWS = MOE_TILES * TMOE


def _route(x, r_ref, base, route_ref, gate_ref):
    n = x.shape[0]
    lane = lax.broadcasted_iota(jnp.int32, (n, LANES), 1)
    logits = jnp.dot(x, r_ref[...], preferred_element_type=F32, precision=lax.Precision.HIGHEST)
    logits = jnp.where(lane < N_EXPERTS, logits, -jnp.inf)
    v1 = jnp.max(logits, -1, keepdims=True)
    i1 = jnp.min(jnp.where(logits == v1, lane, LANES), -1, keepdims=True)
    rest = jnp.where(lane == i1, -jnp.inf, logits)
    v2 = jnp.max(rest, -1, keepdims=True)
    i2 = jnp.min(jnp.where(rest == v2, lane, LANES), -1, keepdims=True)
    e2 = jnp.exp(v2 - v1)
    g1 = 1.0 / (1.0 + e2)
    oh1 = jnp.where(lane == i1, 1.0, 0.0)
    oh2 = jnp.where(lane == i2, 1.0, 0.0)
    tr = lax.broadcasted_iota(jnp.int32, (n, n), 0)
    tc = lax.broadcasted_iota(jnp.int32, (n, n), 1)
    before = jnp.where(tc < tr, 1.0, 0.0).astype(BF16)
    cum1 = jnp.dot(before, oh1.astype(BF16), preferred_element_type=F32)
    cum2 = jnp.dot(before, oh2.astype(BF16), preferred_element_type=F32)
    tot1 = jnp.sum(oh1, axis=0, keepdims=True)
    tot2 = jnp.sum(oh2, axis=0, keepdims=True)
    b = base[...]
    rank1 = jnp.sum(oh1 * (cum1 + b), -1, keepdims=True).astype(jnp.int32)
    rank2 = jnp.sum(oh2 * (cum2 + b + tot1), -1, keepdims=True).astype(jnp.int32)
    base[...] = b + tot1 + tot2
    route_ref[0:n, :] = jnp.where(lane == 0, i1, jnp.where(lane == 1, i2, jnp.where(lane == 2, rank1,
                                                                                    jnp.where(lane == 3, rank2, 0))))
    gate_ref[0:n, :] = jnp.where(lane == 0, g1, jnp.where(lane == 1, e2 * g1, 0.0))


def _moe_head_kernel(mixp_ref, mixd_ref, xp_ref, xd_ref, wo_ref, g1_ref, b1_ref, r_ref,
                     x1p_ref, x1d_ref, route_ref, gate_ref, cnt_ref, base):
    @pl.when(pl.program_id(0) == 0)
    def _():
        base[...] = jnp.zeros_like(base)

    def head(mix_ref, x_ref, x1_ref):
        x1 = _mixer_residual_norm(mix_ref, x_ref, wo_ref, g1_ref, b1_ref)
        x1_ref[...] = x1
        _route(x1, r_ref, base, route_ref, gate_ref)

    _on_token_tile(head, (mixp_ref, xp_ref, x1p_ref), (mixd_ref, xd_ref, x1d_ref))
    cnt_ref[...] = jnp.broadcast_to(base[...], cnt_ref.shape).astype(jnp.int32)


def _moe_head(mix_p, mix_d, x_p, x_d, wo, g1, b1, router):
    return pl.pallas_call(
        _moe_head_kernel,
        grid=(N_TILES,),
        in_specs=[_p_spec(D_MIX), _d_spec(D_MIX), _p_spec(D_MODEL), _d_spec(D_MODEL), _const_spec(wo),
                  _const_spec(g1), _const_spec(b1), _const_spec(router)],
        out_specs=[_p_spec(D_MODEL), _d_spec(D_MODEL), _tile_spec(LANES), _tile_spec(LANES),
                   pl.BlockSpec((8, LANES), lambda i: (0, 0))],
        out_shape=[jax.ShapeDtypeStruct((N_PROMPT, D_MODEL), F32), jax.ShapeDtypeStruct((DEC_BATCH, D_MODEL), F32),
                   jax.ShapeDtypeStruct((N_TOK, LANES), jnp.int32), jax.ShapeDtypeStruct((N_TOK, LANES), F32),
                   jax.ShapeDtypeStruct((8, LANES), jnp.int32)],
        scratch_shapes=[pltpu.VMEM((1, LANES), F32)],
        compiler_params=TOKEN_PARAMS, name="moe_head",
    )(mix_p, mix_d, x_p, x_d, wo, g1, b1, router)


def _row_copy(src, src_row, dst, dst_row, sem):
    return pltpu.make_async_copy(src.at[pl.ds(src_row, 1)], dst.at[pl.ds(dst_row, 1)], sem)


DMA_UNROLL = 8


def _dispatch_kernel(pos1_ref, pos2_ref, cnt_ref, off_ref, pad_ref, xp_ref, xd_ref, wg_ref, wu_ref, wd_ref,
                     xs_hbm, wgb_ref, wub_ref, wdb_ref, sem):
    start = pl.program_id(0) * TM

    def scatter(x_ref, convert_weights):
        n = x_ref.shape[0]
        if convert_weights:
            wgb_ref[...] = wg_ref[...].astype(BF16)
            wub_ref[...] = wu_ref[...].astype(BF16)
            wdb_ref[...] = wd_ref[...].astype(BF16)

        def issue(r, carry):
            t = start + r
            _row_copy(x_ref, r, xs_hbm, pos1_ref[t], sem).start()
            _row_copy(x_ref, r, xs_hbm, pos2_ref[t], sem).start()
            return carry

        lax.fori_loop(0, n, issue, 0, unroll=DMA_UNROLL)
        for _ in range(TOP_K):
            pltpu.make_async_copy(x_ref, xs_hbm.at[pl.ds(0, n)], sem).wait()

    def scatter_and_pad(x_ref):
        scatter(x_ref, False)

        def drain(q, carry):
            _row_copy(x_ref, 0, xs_hbm, 0, sem).wait()
            return carry

        for e in range(N_EXPERTS):
            lo = off_ref[e] + cnt_ref[e]

            def fill(q, carry, lo=lo):
                _row_copy(x_ref, 0, xs_hbm, lo + q, sem).start()
                return carry

            lax.fori_loop(0, pad_ref[e], fill, 0)
            lax.fori_loop(0, pad_ref[e], drain, 0)

    i = pl.program_id(0)

    @pl.when(i < N_PROMPT_TILES)
    def _():
        scatter(xp_ref, True)

    @pl.when(i == N_PROMPT_TILES)
    def _():
        scatter_and_pad(xd_ref)


def _dispatch(x_p, x_d, pos1, pos2, counts, offs, pads, wg, wu, wd):
    flat = [w.reshape(-1, w.shape[-1]) for w in (wg, wu, wd)]
    cast_specs = [_cast_spec(*w.shape, N_PROMPT_TILES) for w in flat]
    outs = pl.pallas_call(
        _dispatch_kernel,
        grid_spec=pltpu.PrefetchScalarGridSpec(
            num_scalar_prefetch=5, grid=(N_TILES,),
            in_specs=[_p_spec(D_MODEL), _d_spec(D_MODEL)] + cast_specs,
            out_specs=[pl.BlockSpec(memory_space=pl.ANY)] + cast_specs,
            scratch_shapes=[pltpu.SemaphoreType.DMA(())]),
        out_shape=[jax.ShapeDtypeStruct((MOE_ROWS, D_MODEL), F32)]
                  + [jax.ShapeDtypeStruct(w.shape, BF16) for w in flat],
        compiler_params=TOKEN_PARAMS, name="moe_dispatch",
    )(pos1, pos2, counts, offs, pads, x_p, x_d, *flat)
    return (outs[0],) + tuple(o.reshape(w.shape) for o, w in zip(outs[1:], (wg, wu, wd)))


def _gffn_kernel(te_ref, nv_ref, xs_ref, wg_ref, wu_ref, wd_ref, ys_ref):
    @pl.when(pl.program_id(0) < nv_ref[0])
    def _():
        ys_ref[...] = _swiglu(xs_ref[...].astype(BF16), wg_ref[0], wu_ref[0], wd_ref[0])


def _gffn(xs, tile_expert, n_valid, wg, wu, wd):
    def tile(i, te, nv):
        return (jnp.minimum(i, nv[0] - 1), 0)

    def expert(i, te, nv):
        return (te[jnp.minimum(i, nv[0] - 1)], 0, 0)

    return pl.pallas_call(
        _gffn_kernel,
        grid_spec=pltpu.PrefetchScalarGridSpec(
            num_scalar_prefetch=2, grid=(MOE_TILES,),
            in_specs=[pl.BlockSpec((TMOE, D_MODEL), tile), pl.BlockSpec((1, D_MODEL, D_FF), expert),
                      pl.BlockSpec((1, D_MODEL, D_FF), expert), pl.BlockSpec((1, D_FF, D_MODEL), expert)],
            out_specs=pl.BlockSpec((TMOE, D_MODEL), tile)),
        out_shape=jax.ShapeDtypeStruct((MOE_ROWS, D_MODEL), F32),
        compiler_params=pltpu.CompilerParams(dimension_semantics=("arbitrary",), vmem_limit_bytes=VMEM_LIMIT),
        name="moe_ffn",
    )(tile_expert, n_valid, xs, wg, wu, wd)


def _combine_kernel(pos1_ref, pos2_ref, xp_ref, xd_ref, gate_ref, ys_hbm, g_ref, b_ref, op_ref, od_ref, rows, sem):
    start = pl.program_id(0) * TM

    def combine(x_ref, o_ref):
        n = x_ref.shape[0]

        def issue(r, carry):
            t = start + r
            pltpu.make_async_copy(ys_hbm.at[pl.ds(pos1_ref[t], 1)], rows.at[0, pl.ds(r, 1)], sem).start()
            pltpu.make_async_copy(ys_hbm.at[pl.ds(pos2_ref[t], 1)], rows.at[1, pl.ds(r, 1)], sem).start()
            return carry

        lax.fori_loop(0, n, issue, 0, unroll=DMA_UNROLL)
        for s in range(TOP_K):
            pltpu.make_async_copy(ys_hbm.at[pl.ds(0, n)], rows.at[s, pl.ds(0, n)], sem).wait()
        gate = gate_ref[0:n, :]
        f = gate[:, 0:1] * rows[0, 0:n, :] + gate[:, 1:2] * rows[1, 0:n, :]
        o_ref[...] = _layernorm(ALPHA * x_ref[...] + f, g_ref[...], b_ref[...])

    _on_token_tile(combine, (xp_ref, op_ref), (xd_ref, od_ref))


def _combine_ln(x_p, x_d, gates, ys, pos1, pos2, g, b):
    return pl.pallas_call(
        _combine_kernel,
        grid_spec=pltpu.PrefetchScalarGridSpec(
            num_scalar_prefetch=2, grid=(N_TILES,),
            in_specs=[_p_spec(D_MODEL), _d_spec(D_MODEL), _tile_spec(LANES), pl.BlockSpec(memory_space=pl.ANY),
                      _const_spec(g), _const_spec(b)],
            out_specs=[_p_spec(D_MODEL), _d_spec(D_MODEL)],
            scratch_shapes=[pltpu.VMEM((TOP_K, TM, D_MODEL), F32), pltpu.SemaphoreType.DMA(())]),
        out_shape=[jax.ShapeDtypeStruct((N_PROMPT, D_MODEL), F32), jax.ShapeDtypeStruct((DEC_BATCH, D_MODEL), F32)],
        compiler_params=TOKEN_PARAMS, name="moe_combine_ln",
    )(pos1, pos2, x_p, x_d, gates, ys, g, b)


def _moe_tail(mix_p, mix_d, x_p, x_d, wo, g1, b1, router, wg, wu, wd, g2, b2):
    x1_p, x1_d, route, gates, cnt = _moe_head(mix_p, mix_d, x_p, x_d, wo, g1, b1, router)
    counts = cnt[0, :N_EXPERTS]
    padded = (counts + TMOE - 1) // TMOE * TMOE
    ends = jnp.cumsum(padded)
    offs = ends - padded
    pos1 = jnp.take(offs, route[:, 0]) + route[:, 2]
    pos2 = jnp.take(offs, route[:, 1]) + route[:, 3]
    tile_ends = ends // TMOE
    tile_ids = jnp.arange(MOE_TILES, dtype=jnp.int32)
    tile_expert = jnp.minimum(jnp.sum(tile_ids[:, None] >= tile_ends[None, :], axis=1), N_EXPERTS - 1)
    xs, wg_b, wu_b, wd_b = _dispatch(x1_p, x1_d, pos1, pos2, counts, offs, padded - counts, wg, wu, wd)
    ys = _gffn(xs, tile_expert.astype(jnp.int32), tile_ends[N_EXPERTS - 1:], wg_b, wu_b, wd_b)
    return _combine_ln(x1_p, x1_d, gates, ys, pos1, pos2, g2, b2)


def _rope_tables(pos):
    half = HEAD_DIM // 2
    inv_freq = ROPE_THETA ** (-jnp.arange(half, dtype=F32) / half)
    ang = pos.astype(F32)[:, None] * inv_freq[None, :]
    cos = jnp.cos(ang)
    sin = jnp.sin(ang)
    return jnp.concatenate([cos] * 4, axis=1), jnp.concatenate([-sin, sin, -sin, sin], axis=1)


def _pad_lanes(a):
    return jnp.pad(a, [(0, 0)] * (a.ndim - 1) + [(0, LANES - a.shape[-1])])


def kernel(x_prompt, x_sample, cache_win_k, cache_win_v, state_ssd, state_ssd_conv, state_conv, w_in, attn_sinks,
           ssd_conv_w, ssd_conv_b, ssd_dt_bias, ssd_a_log, ssd_d, ssd_norm_w, sconv_w, w_out, ln1_g, ln1_b, ln2_g,
           ln2_b, ffn_w_gate, ffn_w_up, ffn_w_down, moe_router, moe_w_gate, moe_w_up, moe_w_down):
    x_p = x_prompt.reshape(N_PROMPT, D_MODEL)
    x_d = x_sample.reshape(DEC_BATCH, D_MODEL)
    cos_p, sin_p = _rope_tables(jnp.arange(SEQ, dtype=jnp.int32))
    cos_s, sin_s = _rope_tables(PAST_LEN + jnp.arange(1, dtype=jnp.int32))
    split = C_GB + SSD_HEADS
    w_in_r = jnp.concatenate([w_in[:, :, :C_GB], w_in[:, :, split:], _pad_lanes(w_in[:, :, C_GB:split])],
                             axis=-1).astype(BF16)
    w_out_b = w_out.astype(BF16)
    dtb = _pad_lanes(ssd_dt_bias)[:, None, :]
    alog = _pad_lanes(ssd_a_log)[:, None, :]
    dskip = jnp.repeat(ssd_d, SSD_HEAD_DIM, axis=-1)[:, None, :]
    router = _pad_lanes(moe_router)
    sink_rows = jnp.broadcast_to(attn_sinks[:, :, None], (DEPTH, N_HEADS, LANES))
    kc = cache_win_k.reshape(DEPTH, DEC_BATCH, WINDOW, D_KV)
    vc = cache_win_v.reshape(DEPTH, DEC_BATCH, WINDOW, D_KV)
    st = state_ssd.reshape(DEPTH, DEC_BATCH, SSD_HEADS * D_STATE_FLAT)
    sbuf = state_ssd_conv.reshape(DEPTH, DEC_BATCH, (SSD_CONV - 1) * D_XBC)
    cbuf = state_conv.reshape(DEPTH, DEC_BATCH, (CONV_WIDTH - 1) * D_CONV)

    outs = {n: [] for n in ("pk", "pv", "ph", "psc", "pc", "sk", "sv", "sh", "ssc", "sc")}
    for i in range(DEPTH):
        j = i // 2
        dense = i % 2 == 0
        if dense:
            proj_p, proj_d, wg_b, wu_b, wd_b = _inproj(x_p, x_d, w_in_r[i],
                                                        (ffn_w_gate[j], ffn_w_up[j], ffn_w_down[j]))
        else:
            proj_p, proj_d = _inproj(x_p, x_d, w_in_r[i])
        proj_p = proj_p.reshape(BATCH, SEQ, D_PROJ)
        params = (ssd_conv_w[i], ssd_conv_b[i][None, :], dtb[i], alog[i], dskip[i], ssd_norm_w[i][None, :],
                  sconv_w[i])
        mix_p, klast, hlast, culast = _mixer_prompt(proj_p, attn_sinks[i], cos_p, sin_p, *params)
        oattn, knew = _dec_attn(proj_d, kc[i], vc[i], cos_s, sin_s, sink_rows[i])
        mix_d, stnew, sbufnew, cbufnew = _dec_ssd(oattn, proj_d, sbuf[i], cbuf[i], st[i], *params)
        mix_p = mix_p.reshape(N_PROMPT, D_MIX)
        ln1 = (ln1_g[i][None, :], ln1_b[i][None, :])
        ln2 = (ln2_g[i][None, :], ln2_b[i][None, :])
        if dense:
            x_p, x_d = _dense_tail(mix_p, mix_d, x_p, x_d, w_out_b[i], *ln1, wg_b, wu_b, wd_b, *ln2)
        else:
            x_p, x_d = _moe_tail(mix_p, mix_d, x_p, x_d, w_out_b[i], *ln1, router[j],
                                 moe_w_gate[j], moe_w_up[j], moe_w_down[j], *ln2)

        def tail(n_rows, col, width):
            return proj_p[:, SEQ - n_rows:, col:col + width]

        outs["pk"].append(klast.reshape(BATCH, WINDOW, N_KV_HEADS, HEAD_DIM))
        outs["pv"].append(tail(WINDOW, C_V, D_KV).reshape(BATCH, WINDOW, N_KV_HEADS, HEAD_DIM))
        outs["ph"].append(hlast)
        outs["psc"].append(tail(SSD_CONV - 1, C_XBC, D_XBC))
        outs["pc"].append(culast[:, 8 - (CONV_WIDTH - 1):, :])
        vnew = proj_d[:, C_V:C_V + D_KV]
        outs["sk"].append(jnp.concatenate([kc[i][:, 1:], knew[:, None, :]], axis=1)
                          .reshape(DEC_BATCH, WINDOW, N_KV_HEADS, HEAD_DIM))
        outs["sv"].append(jnp.concatenate([vc[i][:, 1:], vnew[:, None, :]], axis=1)
                          .reshape(DEC_BATCH, WINDOW, N_KV_HEADS, HEAD_DIM))
        outs["sh"].append(stnew.reshape(DEC_BATCH, SSD_HEADS, SSD_HEAD_DIM, SSD_STATE))
        outs["ssc"].append(sbufnew.reshape(DEC_BATCH, SSD_CONV - 1, D_XBC))
        outs["sc"].append(cbufnew.reshape(DEC_BATCH, CONV_WIDTH - 1, D_CONV))
    y_prompt = x_p.reshape(BATCH, SEQ, D_MODEL)
    y_sample = x_d.reshape(DEC_BATCH, 1, D_MODEL)
    stk = lambda n: jnp.stack(outs[n])
    return (y_prompt, y_sample, stk("pk"), stk("pv"), stk("ph"), stk("psc"), stk("pc"),
            stk("sk"), stk("sv"), stk("sh"), stk("ssc"), stk("sc"))
```

```python
import functools
import math

import jax
import jax.numpy as jnp
from jax import lax
from jax.experimental import pallas as pl
from jax.experimental.pallas import tpu as pltpu

D_MODEL = 1024
BATCH = 2
SEQ = 8192
DEPTH = 4
DEC_BATCH = 128
PAST_LEN = 8192
N_HEADS = 8
N_KV_HEADS = 2
HEAD_DIM = 64
Q_PER_KV = N_HEADS // N_KV_HEADS
D_ATTN = N_HEADS * HEAD_DIM
D_KV = N_KV_HEADS * HEAD_DIM
WINDOW = 128
BLK = 128
ROPE_THETA = 10000.0
SSD_HEADS = 4
SSD_HEAD_DIM = 64
D_SSD = SSD_HEADS * SSD_HEAD_DIM
SSD_GROUPS = 2
SSD_STATE = 64
SSD_CONV = 4
D_XBC = D_SSD + 2 * SSD_GROUPS * SSD_STATE
D_CONV = 256
CONV_WIDTH = 3
D_MIX = D_ATTN + D_SSD + D_CONV
D_FF = 2816
N_EXPERTS = 8
ALPHA = (2 * DEPTH) ** 0.25
NORM_EPS = 1e-5

LANES = 128
N_PROMPT = BATCH * SEQ
N_TOK = N_PROMPT + DEC_BATCH
N_BLK = SEQ // BLK
TM = 512
N_TILES = pl.cdiv(N_TOK, TM)
VMEM_LIMIT = 56 * 1024 * 1024

C_Q, C_K, C_V, C_Z, C_XBC = 0, 512, 640, 768, 1024
C_GB, C_GC, C_U, C_DT = 1536, 1792, 2048, 2304
D_PROJ = C_DT + LANES

BF16 = jnp.bfloat16
F32 = jnp.float32


def _sigmoid(x):
    return 1.0 / (1.0 + jnp.exp(-x))


def _silu(x):
    return x * _sigmoid(x)


def _softplus(x):
    return jnp.maximum(x, 0.0) + jnp.log1p(jnp.exp(-jnp.abs(x)))


def _layernorm(xf, g, b):
    mu = jnp.mean(xf, -1, keepdims=True)
    xc = xf - mu
    var = jnp.mean(xc * xc, -1, keepdims=True)
    return xc * lax.rsqrt(var + NORM_EPS) * g + b


def _rope(x, cos, sin):
    w = x.shape[1]
    reps = w // LANES
    if reps > 1:
        cos = jnp.concatenate([cos] * reps, axis=1)
        sin = jnp.concatenate([sin] * reps, axis=1)
    lane = lax.broadcasted_iota(jnp.int32, x.shape, 1)
    first_half = (lane % HEAD_DIM) < (HEAD_DIM // 2)
    partner = jnp.where(first_half, pltpu.roll(x, w - HEAD_DIM // 2, 1), pltpu.roll(x, HEAD_DIM // 2, 1))
    return x * cos + partner * sin


def _dot(a, b):
    return jnp.dot(a.astype(BF16), b.astype(BF16), preferred_element_type=F32)


def _dot_nt(a, b):
    return lax.dot_general(a.astype(BF16), b.astype(BF16), (((1,), (1,)), ((), ())), preferred_element_type=F32)


def _dot_tn(a, b):
    return lax.dot_general(a.astype(BF16), b.astype(BF16), (((0,), (0,)), ((), ())), preferred_element_type=F32)


N_PROMPT_TILES = N_PROMPT // TM
TOKEN_PARAMS = pltpu.CompilerParams(dimension_semantics=("arbitrary",), vmem_limit_bytes=VMEM_LIMIT)
WHOLE_VMEM = pl.BlockSpec(memory_space=pltpu.VMEM)


def _p_spec(width):
    return pl.BlockSpec((TM, width), lambda i, *_: (jnp.minimum(i, N_PROMPT_TILES - 1), 0))


def _d_spec(width):
    return pl.BlockSpec((DEC_BATCH, width), lambda i, *_: (0, 0))


def _tile_spec(width):
    return pl.BlockSpec((TM, width), lambda i, *_: (i, 0))


def _const_spec(a):
    return pl.BlockSpec(a.shape, lambda i, *_: (0,) * a.ndim)


def _layer_spec(a, layer):
    return pl.BlockSpec((1,) + a.shape[1:], lambda i, *_: (layer,) + (0,) * (a.ndim - 1))


def _on_token_tile(fn, prompt_refs, decode_refs):
    i = pl.program_id(0)

    @pl.when(i < N_PROMPT_TILES)
    def _():
        fn(*prompt_refs)

    @pl.when(i == N_PROMPT_TILES)
    def _():
        fn(*decode_refs)


def _cast_specs(w, layer, steps):
    _, rows, cols = w.shape
    return (pl.BlockSpec((1, rows // steps, cols), lambda i, *_: (layer, jnp.minimum(i, steps - 1), 0)),
            pl.BlockSpec((rows // steps, cols), lambda i, *_: (jnp.minimum(i, steps - 1), 0)))


FFN_CAST_STEPS = 16


def _inproj_kernel(xp_ref, xd_ref, w_ref, *refs):
    if len(refs) == 2:
        pp_ref, pd_ref = refs
    else:
        wg_ref, wu_ref, wd_ref, pp_ref, pd_ref, wgb_ref, wub_ref, wdb_ref = refs

    def project(x_ref, o_ref):
        o_ref[...] = jnp.dot(x_ref[...].astype(BF16), w_ref[0], preferred_element_type=F32)

    _on_token_tile(project, (xp_ref, pp_ref), (xd_ref, pd_ref))
    if len(refs) > 2:
        @pl.when(pl.program_id(0) < FFN_CAST_STEPS)
        def _():
            wgb_ref[...] = wg_ref[0].astype(BF16)
            wub_ref[...] = wu_ref[0].astype(BF16)
            wdb_ref[...] = wd_ref[0].astype(BF16)


def _inproj(x_p, x_d, w, layer, ffn_w=None, ffn_layer=0):
    in_specs = [_p_spec(D_MODEL), _d_spec(D_MODEL), _layer_spec(w, layer)]
    out_specs = [_p_spec(D_PROJ), _d_spec(D_PROJ)]
    out_shape = [jax.ShapeDtypeStruct((N_PROMPT, D_PROJ), F32), jax.ShapeDtypeStruct((DEC_BATCH, D_PROJ), F32)]
    args = [x_p, x_d, w]
    if ffn_w is not None:
        specs = [_cast_specs(a, ffn_layer, FFN_CAST_STEPS) for a in ffn_w]
        in_specs += [s[0] for s in specs]
        out_specs += [s[1] for s in specs]
        out_shape += [jax.ShapeDtypeStruct(a.shape[1:], BF16) for a in ffn_w]
        args += list(ffn_w)
    return pl.pallas_call(
        _inproj_kernel, grid=(N_TILES,), in_specs=in_specs, out_specs=out_specs, out_shape=out_shape,
        compiler_params=TOKEN_PARAMS, name="inproj",
    )(*args)


def _ssd_gate_norm(y, z, norm_w):
    y = y * _silu(z)
    return y * lax.rsqrt(jnp.mean(y * y, -1, keepdims=True) + NORM_EPS) * norm_w


def _attention_block(b, sinks_ref, q, k, v, bias, kbd, vbd, mix_ref):
    lo = lax.broadcasted_iota(jnp.int32, (BLK, LANES), 1) < HEAD_DIM
    k_sw = pltpu.roll(k, HEAD_DIM, 1)
    v_sw = pltpu.roll(v, HEAD_DIM, 1)
    for g in range(N_KV_HEADS):
        for ref, x, x_sw in ((kbd, k, k_sw), (vbd, v, v_sw)):
            a_src, b_src = (x, x_sw) if g == 0 else (x_sw, x)
            ref[b, g, 0:BLK, :] = ref[b, g, BLK:2 * BLK, :]
            ref[b, g, 2 * BLK:3 * BLK, :] = ref[b, g, 3 * BLK:4 * BLK, :]
            ref[b, g, BLK:2 * BLK, :] = jnp.where(lo, a_src, 0.0).astype(BF16)
            ref[b, g, 3 * BLK:4 * BLK, :] = jnp.where(lo, 0.0, b_src).astype(BF16)
    for m in range(N_HEADS // 2):
        g = (2 * m) // Q_PER_KV
        s = _dot_nt(q[:, m * LANES:(m + 1) * LANES], kbd[b, g]) + bias
        probs, inv = [], []
        for hh in range(2):
            sh = s[:, hh * 2 * BLK:(hh + 1) * 2 * BLK]
            sink = sinks_ref[2 * m + hh]
            mx = jnp.maximum(jnp.max(sh, -1, keepdims=True), sink)
            p = jnp.exp(sh - mx)
            inv.append(1.0 / (jnp.sum(p, -1, keepdims=True) + jnp.exp(sink - mx)))
            probs.append(p.astype(BF16))
        o = jnp.dot(jnp.concatenate(probs, axis=1), vbd[b, g], preferred_element_type=F32)
        mix_ref[b, :, m * LANES:(m + 1) * LANES] = (o * jnp.where(lo, inv[0], inv[1])).astype(mix_ref.dtype)


def _mixer_prompt_kernel(sinks_ref, q_ref, k_ref, v_ref, z_ref, xbc_ref, gb_ref, gc_ref, u_ref, dt_ref,
                         cos_ref, sin_ref, bias_ref, cw_ref, cb_ref, dtb_ref, alog_ref, dskip_ref, nw_ref, scw_ref,
                         mix_ref, klast_ref, hlast_ref, culast_ref,
                         kbd, vbd, conv_ext, cu_ext, hstate):
    j = pl.program_id(0)

    @pl.when(j == 0)
    def _():
        kbd[...] = jnp.zeros_like(kbd)
        vbd[...] = jnp.zeros_like(vbd)
        conv_ext[:, 0:8, :] = jnp.zeros((BATCH, 8, D_XBC), F32)
        cu_ext[:, 0:8, :] = jnp.zeros((BATCH, 8, D_CONV), F32)
        hstate[...] = jnp.zeros_like(hstate)

    cos = cos_ref[...]
    sin = sin_ref[...]
    bias = bias_ref[jnp.minimum(j, 1)]
    for b in range(BATCH):
        q = _rope(q_ref[b], cos, sin) * (1.0 / math.sqrt(HEAD_DIM))
        k = _rope(k_ref[b], cos, sin)
        klast_ref[b] = k
        _attention_block(b, sinks_ref, q, k, v_ref[b], bias, kbd, vbd, mix_ref)
        _ssd_block(b, z_ref, xbc_ref, dt_ref, cw_ref, cb_ref, dtb_ref, alog_ref, dskip_ref, nw_ref,
                   mix_ref, hlast_ref, conv_ext, hstate)
        _conv_block(b, gb_ref, gc_ref, u_ref, scw_ref, mix_ref, culast_ref, cu_ext)


def _ssd_block(b, z_ref, xbc_ref, dt_ref, cw_ref, cb_ref, dtb_ref, alog_ref, dskip_ref, nw_ref,
               mix_ref, hlast_ref, conv_ext, hstate):
    conv_ext[b, 8:8 + BLK, :] = xbc_ref[b]
    acc = conv_ext[b, 5:5 + BLK, :] * cw_ref[0:1, :]
    for t in range(1, SSD_CONV):
        acc = acc + conv_ext[b, 5 + t:5 + t + BLK, :] * cw_ref[t:t + 1, :]
    conv_ext[b, 0:8, :] = conv_ext[b, BLK:BLK + 8, :]
    xbc = _silu(acc + cb_ref[...])
    xs = xbc[:, 0:D_SSD]
    bm = xbc[:, D_SSD:D_SSD + LANES]
    cm = xbc[:, D_SSD + LANES:D_SSD + 2 * LANES]
    dt = _softplus(dt_ref[b] + dtb_ref[...])
    dta = dt * (-jnp.exp(alog_ref[...]))
    ti = lax.broadcasted_iota(jnp.int32, (BLK, BLK), 0)
    si = lax.broadcasted_iota(jnp.int32, (BLK, BLK), 1)
    causal = si <= ti
    cum = jnp.dot(causal.astype(F32), dta, preferred_element_type=F32, precision=lax.Precision.HIGHEST)
    cum_t = cum.T
    ys = []
    for g in range(SSD_GROUPS):
        sl = slice(g * SSD_STATE, (g + 1) * SSD_STATE)
        b_g = bm[:, sl]
        c_g = cm[:, sl]
        cb_scores = _dot_nt(c_g, b_g)
        for hh in range(SSD_HEADS // SSD_GROUPS):
            h = g * (SSD_HEADS // SSD_GROUPS) + hh
            ccol = cum[:, h:h + 1]
            crow = cum_t[h:h + 1, :]
            clast = cum[BLK - 1:BLK, h:h + 1]
            decay = jnp.exp(jnp.where(causal, ccol - crow, -jnp.inf))
            x_h = xs[:, h * SSD_HEAD_DIM:(h + 1) * SSD_HEAD_DIM]
            xdt = x_h * dt[:, h:h + 1]
            h_prev = hstate[b, h]
            y = _dot(cb_scores * decay, xdt)
            y = y + _dot_nt(c_g, h_prev) * jnp.exp(ccol)
            states = _dot_tn(xdt * jnp.exp(clast - ccol), b_g)
            hstate[b, h] = h_prev * jnp.exp(clast) + states
            ys.append(y)
    y = jnp.concatenate(ys, axis=1) + dskip_ref[...] * xs
    mix_ref[b, :, D_ATTN:D_ATTN + D_SSD] = _ssd_gate_norm(y, z_ref[b], nw_ref[...]).astype(mix_ref.dtype)
    hlast_ref[b] = hstate[b]


def _conv_block(b, gb_ref, gc_ref, u_ref, scw_ref, mix_ref, culast_ref, cu_ext):
    cu_ext[b, 8:8 + BLK, :] = gc_ref[b] * u_ref[b]
    cc = cu_ext[b, 6:6 + BLK, :] * scw_ref[0:1, :]
    for t in range(1, CONV_WIDTH):
        cc = cc + cu_ext[b, 6 + t:6 + t + BLK, :] * scw_ref[t:t + 1, :]
    tail = cu_ext[b, BLK:BLK + 8, :]
    cu_ext[b, 0:8, :] = tail
    culast_ref[b] = tail
    mix_ref[b, :, D_ATTN + D_SSD:D_MIX] = (gb_ref[b] * cc).astype(mix_ref.dtype)


def _attn_bias():
    r = jnp.arange(BLK, dtype=jnp.int32)[:, None]
    c = jnp.arange(4 * BLK, dtype=jnp.int32)[None, :] % (2 * BLK)
    window = (c >= r) & (c - BLK <= r)
    first = window & (c >= BLK)
    return jnp.where(jnp.stack([first, window]), 0.0, -jnp.inf).astype(F32)


def _mixer_prompt(proj, sinks, cos_tab, sin_tab, cw, cb, dtb, alog, dskip, nw, scw):
    def pblk(width, col):
        return pl.BlockSpec((BATCH, BLK, width), lambda j, *_: (0, j, col // width))

    def full(a):
        return pl.BlockSpec(a.shape, lambda j, *_: (0,) * a.ndim)

    bias = _attn_bias()
    params = (cw, cb, dtb, alog, dskip, nw, scw)
    out_shape = [jax.ShapeDtypeStruct((BATCH, SEQ, D_MIX), BF16),
                 jax.ShapeDtypeStruct((BATCH, BLK, D_KV), F32),
                 jax.ShapeDtypeStruct((BATCH, SSD_HEADS, SSD_HEAD_DIM, SSD_STATE), F32),
                 jax.ShapeDtypeStruct((BATCH, 8, D_CONV), F32)]
    grid_spec = pltpu.PrefetchScalarGridSpec(
        num_scalar_prefetch=1,
        grid=(N_BLK,),
        in_specs=[pblk(D_ATTN, C_Q), pblk(D_KV, C_K), pblk(D_KV, C_V), pblk(D_SSD, C_Z), pblk(D_XBC, C_XBC),
                  pblk(D_CONV, C_GB), pblk(D_CONV, C_GC), pblk(D_CONV, C_U), pblk(LANES, C_DT),
                  pl.BlockSpec((BLK, LANES), lambda j, *_: (j, 0)),
                  pl.BlockSpec((BLK, LANES), lambda j, *_: (j, 0)), full(bias)] + [full(a) for a in params],
        out_specs=[pl.BlockSpec((BATCH, BLK, D_MIX), lambda j, *_: (0, j, 0))] + [full(o) for o in out_shape[1:]],
        scratch_shapes=[pltpu.VMEM((BATCH, N_KV_HEADS, 4 * BLK, LANES), BF16),
                        pltpu.VMEM((BATCH, N_KV_HEADS, 4 * BLK, LANES), BF16),
                        pltpu.VMEM((BATCH, BLK + 8, D_XBC), F32), pltpu.VMEM((BATCH, BLK + 8, D_CONV), F32),
                        pltpu.VMEM((BATCH, SSD_HEADS, SSD_HEAD_DIM, SSD_STATE), F32)],
    )
    return pl.pallas_call(
        _mixer_prompt_kernel,
        grid_spec=grid_spec,
        out_shape=out_shape,
        compiler_params=pltpu.CompilerParams(dimension_semantics=("arbitrary",), vmem_limit_bytes=VMEM_LIMIT),
        name="mixer_prompt",
    )(sinks, *([proj] * 9), cos_tab, sin_tab, bias, *params)


DEC_BB = 32


def _dec_attn_kernel(q_ref, k_ref, v_ref, kc_ref, vc_ref, cos_ref, sin_ref, sink_ref,
                     o_ref, knew_ref, qbd, kn8, vn8, o_scr):
    cos = cos_ref[...]
    sin = sin_ref[...]
    q = _rope(q_ref[...], cos, sin) * (1.0 / math.sqrt(HEAD_DIM))
    kn = _rope(k_ref[...], cos, sin)
    vn = v_ref[...]
    knew_ref[...] = kn
    lane = lax.broadcasted_iota(jnp.int32, (DEC_BB, LANES), 1)
    for h in range(N_HEADS):
        g = h // Q_PER_KV
        pair = q[:, (h // 2) * LANES:(h // 2 + 1) * LANES]
        if h % 2 != g:
            pair = pltpu.roll(pair, HEAD_DIM, 1)
        qbd[pl.ds(h, DEC_BB, stride=8), :] = jnp.where(lane // HEAD_DIM == g, pair, 0.0)
        kn8[pl.ds(h, DEC_BB, stride=8), :] = kn
        vn8[pl.ds(h, DEC_BB, stride=8), :] = vn
    q3 = qbd[...].reshape(DEC_BB, N_HEADS, LANES)
    kn3 = kn8[...].reshape(DEC_BB, N_HEADS, LANES)
    vn3 = vn8[...].reshape(DEC_BB, N_HEADS, LANES)
    s3 = jnp.einsum('bhl,bsl->bhs', q3.astype(BF16), kc_ref[0].astype(BF16), preferred_element_type=F32)
    s_self = jnp.sum(q3 * kn3, axis=-1, keepdims=True)
    sink3 = sink_ref[...][None, :, 0:1]
    m = jnp.maximum(jnp.maximum(jnp.max(s3, -1, keepdims=True), s_self), sink3)
    p3 = jnp.exp(s3 - m)
    p_self = jnp.exp(s_self - m)
    denom = jnp.sum(p3, -1, keepdims=True) + p_self + jnp.exp(sink3 - m)
    o3 = jnp.einsum('bhs,bsl->bhl', p3.astype(BF16), vc_ref[0].astype(BF16), preferred_element_type=F32)
    o3 = (o3 + p_self * vn3) * (1.0 / denom)
    h3 = lax.broadcasted_iota(jnp.int32, o3.shape, 1)
    l3 = lax.broadcasted_iota(jnp.int32, o3.shape, 2)
    o3 = jnp.where(l3 // HEAD_DIM == h3 // Q_PER_KV, o3, 0.0)
    o_scr[...] = o3.reshape(DEC_BB * N_HEADS, LANES)
    for pair in range(N_HEADS // 2):
        g = (2 * pair) // Q_PER_KV
        a = o_scr[pl.ds(2 * pair, DEC_BB, stride=8), :]
        b = o_scr[pl.ds(2 * pair + 1, DEC_BB, stride=8), :]
        if g == 1:
            a = pltpu.roll(a, HEAD_DIM, 1)
        else:
            b = pltpu.roll(b, HEAD_DIM, 1)
        o_ref[:, pair * LANES:(pair + 1) * LANES] = a + b


def _dec_attn(proj, kc, vc, layer, cos1, sin1, sink_rows):
    def pblk(width, col):
        return pl.BlockSpec((DEC_BB, width), lambda i: (i, col // width))

    def full(a):
        return pl.BlockSpec(a.shape, lambda i: (0,) * a.ndim)

    return pl.pallas_call(
        _dec_attn_kernel,
        grid=(DEC_BATCH // DEC_BB,),
        in_specs=[pblk(D_ATTN, C_Q), pblk(D_KV, C_K), pblk(D_KV, C_V),
                  pl.BlockSpec((1, DEC_BB, WINDOW, D_KV), lambda i: (layer, i, 0, 0)),
                  pl.BlockSpec((1, DEC_BB, WINDOW, D_KV), lambda i: (layer, i, 0, 0)),
                  full(cos1), full(sin1), full(sink_rows)],
        out_specs=[pl.BlockSpec((DEC_BB, D_ATTN), lambda i: (i, 0)),
                   pl.BlockSpec((DEC_BB, D_KV), lambda i: (i, 0))],
        out_shape=[jax.ShapeDtypeStruct((DEC_BATCH, D_ATTN), F32),
                   jax.ShapeDtypeStruct((DEC_BATCH, D_KV), F32)],
        scratch_shapes=[pltpu.VMEM((DEC_BB * N_HEADS, LANES), F32)] * 4,
        compiler_params=pltpu.CompilerParams(dimension_semantics=("arbitrary",), vmem_limit_bytes=VMEM_LIMIT),
        name="dec_attn",
    )(proj, proj, proj, kc, vc, cos1, sin1, sink_rows)


D_STATE_FLAT = SSD_HEAD_DIM * SSD_STATE


def _dec_ssd_kernel(oattn_ref, z_ref, xbc_ref, gb_ref, gc_ref, u_ref, dt_ref, sbuf_ref, cbuf_ref, st_ref,
                    cw_ref, cb_ref, dtb_ref, alog_ref, dskip_ref, nw_ref, scw_ref,
                    mix_ref, stnew_ref, sbufnew_ref, cbufnew_ref,
                    st_t, stnew_t, xdt_t, b_t, c_t, dec_t, y_t):
    xbc_new = xbc_ref[...]
    acc = sbuf_ref[0, :,0:D_XBC] * cw_ref[0:1, :]
    for t in range(1, SSD_CONV - 1):
        acc = acc + sbuf_ref[0, :,t * D_XBC:(t + 1) * D_XBC] * cw_ref[t:t + 1, :]
    acc = acc + xbc_new * cw_ref[SSD_CONV - 1:SSD_CONV, :]
    sbufnew_ref[:, 0:(SSD_CONV - 2) * D_XBC] = sbuf_ref[0, :,D_XBC:(SSD_CONV - 1) * D_XBC]
    sbufnew_ref[:, (SSD_CONV - 2) * D_XBC:(SSD_CONV - 1) * D_XBC] = xbc_new
    xbc = _silu(acc + cb_ref[...])
    xs = xbc[:, 0:D_SSD]
    dt = _softplus(dt_ref[...] + dtb_ref[...])
    dta = dt * (-jnp.exp(alog_ref[...]))
    dec_t[...] = jnp.exp(dta).T
    dt_t = dt.T
    xs_t = xs.T
    for h in range(SSD_HEADS):
        sl = slice(h * SSD_HEAD_DIM, (h + 1) * SSD_HEAD_DIM)
        xdt_t[sl, :] = xs_t[sl, :] * dt_t[h:h + 1, :]
    b_t[...] = xbc[:, D_SSD:D_SSD + LANES].T
    c_t[...] = xbc[:, D_SSD + LANES:D_SSD + 2 * LANES].T
    for h in range(SSD_HEADS):
        g = h // (SSD_HEADS // SSD_GROUPS)
        st_t[...] = st_ref[0, :, h * D_STATE_FLAT:(h + 1) * D_STATE_FLAT].T
        dec = dec_t[h:h + 1, :]
        bg = b_t[g * SSD_STATE:(g + 1) * SSD_STATE, :]
        cg = c_t[g * SSD_STATE:(g + 1) * SSD_STATE, :]

        def body(p, carry, h=h, dec=dec, bg=bg, cg=cg):
            off = pl.multiple_of(p * SSD_STATE, SSD_STATE)
            new = st_t[pl.ds(off, SSD_STATE), :] * dec + xdt_t[pl.ds(h * SSD_HEAD_DIM + p, 1), :] * bg
            stnew_t[pl.ds(off, SSD_STATE), :] = new
            y_t[pl.ds(h * SSD_HEAD_DIM + p, 1), :] = jnp.sum(new * cg, axis=0, keepdims=True)
            return carry

        lax.fori_loop(0, SSD_HEAD_DIM, body, 0)
        stnew_ref[:, h * D_STATE_FLAT:(h + 1) * D_STATE_FLAT] = stnew_t[...].T
    y = y_t[...].T + dskip_ref[...] * xs
    mix_ref[:, 0:D_ATTN] = oattn_ref[...].astype(mix_ref.dtype)
    mix_ref[:, D_ATTN:D_ATTN + D_SSD] = _ssd_gate_norm(y, z_ref[...], nw_ref[...]).astype(mix_ref.dtype)

    cu = gc_ref[...] * u_ref[...]
    cc = cbuf_ref[0, :,0:D_CONV] * scw_ref[0:1, :] + cbuf_ref[0, :,D_CONV:2 * D_CONV] * scw_ref[1:2, :]
    cc = cc + cu * scw_ref[2:3, :]
    cbufnew_ref[:, 0:D_CONV] = cbuf_ref[0, :,D_CONV:2 * D_CONV]
    cbufnew_ref[:, D_CONV:2 * D_CONV] = cu
    mix_ref[:, D_ATTN + D_SSD:D_MIX] = (gb_ref[...] * cc).astype(mix_ref.dtype)


def _dec_ssd(oattn, proj, sbuf, cbuf, st, layer, cw, cb, dtb, alog, dskip, nw, scw):
    def pblk(width, col):
        return pl.BlockSpec((DEC_BATCH, width), lambda i: (0, col // width))

    def full(a):
        return pl.BlockSpec(a.shape, lambda i: (0,) * a.ndim)

    def new(a):
        return pl.BlockSpec(a.shape[1:], lambda i: (0, 0))

    params = (cw, cb, dtb, alog, dskip, nw, scw)
    return pl.pallas_call(
        _dec_ssd_kernel,
        grid=(1,),
        in_specs=[full(oattn), pblk(D_SSD, C_Z), pblk(D_XBC, C_XBC), pblk(D_CONV, C_GB), pblk(D_CONV, C_GC),
                  pblk(D_CONV, C_U), pblk(LANES, C_DT), _layer_spec(sbuf, layer), _layer_spec(cbuf, layer),
                  _layer_spec(st, layer)] + [full(a) for a in params],
        out_specs=[pl.BlockSpec((DEC_BATCH, D_MIX), lambda i: (0, 0)), new(st), new(sbuf), new(cbuf)],
        out_shape=[jax.ShapeDtypeStruct((DEC_BATCH, D_MIX), BF16), jax.ShapeDtypeStruct(st.shape[1:], F32),
                   jax.ShapeDtypeStruct(sbuf.shape[1:], F32), jax.ShapeDtypeStruct(cbuf.shape[1:], F32)],
        scratch_shapes=[pltpu.VMEM((D_STATE_FLAT, DEC_BATCH), F32), pltpu.VMEM((D_STATE_FLAT, DEC_BATCH), F32),
                        pltpu.VMEM((D_SSD, DEC_BATCH), F32), pltpu.VMEM((LANES, DEC_BATCH), F32),
                        pltpu.VMEM((LANES, DEC_BATCH), F32), pltpu.VMEM((LANES, DEC_BATCH), F32),
                        pltpu.VMEM((D_SSD, DEC_BATCH), F32)],
        compiler_params=pltpu.CompilerParams(dimension_semantics=("arbitrary",), vmem_limit_bytes=VMEM_LIMIT),
        name="dec_ssd",
    )(oattn, proj, proj, proj, proj, proj, proj, sbuf, cbuf, st, *params)


def _swiglu(xb, wg, wu, wd):
    gate = jnp.dot(xb, wg, preferred_element_type=F32)
    up = jnp.dot(xb, wu, preferred_element_type=F32)
    return jnp.dot((_silu(gate) * up).astype(BF16), wd, preferred_element_type=F32)


def _mixer_residual_norm(mix_ref, x_ref, wo_ref, g_ref, b_ref):
    m = jnp.dot(mix_ref[...], wo_ref[0], preferred_element_type=F32)
    return _layernorm(ALPHA * x_ref[...] + m, g_ref[...], b_ref[...])


def _dense_tail_kernel(mixp_ref, mixd_ref, xp_ref, xd_ref, wo_ref, g1_ref, b1_ref, wg_ref, wu_ref, wd_ref,
                       g2_ref, b2_ref, op_ref, od_ref):
    def tail(mix_ref, x_ref, o_ref):
        x1 = _mixer_residual_norm(mix_ref, x_ref, wo_ref, g1_ref, b1_ref)
        f = _swiglu(x1.astype(BF16), wg_ref[...], wu_ref[...], wd_ref[...])
        o_ref[...] = _layernorm(ALPHA * x1 + f, g2_ref[...], b2_ref[...])

    _on_token_tile(tail, (mixp_ref, xp_ref, op_ref), (mixd_ref, xd_ref, od_ref))


def _dense_tail(mix_p, mix_d, x_p, x_d, wo, layer, g1, b1, wg, wu, wd, g2, b2):
    return pl.pallas_call(
        _dense_tail_kernel,
        grid=(N_TILES,),
        in_specs=[_p_spec(D_MIX), _d_spec(D_MIX), _p_spec(D_MODEL), _d_spec(D_MODEL), _layer_spec(wo, layer),
                  _const_spec(g1), _const_spec(b1), WHOLE_VMEM, WHOLE_VMEM, WHOLE_VMEM,
                  _const_spec(g2), _const_spec(b2)],
        out_specs=[_p_spec(D_MODEL), _d_spec(D_MODEL)],
        out_shape=[jax.ShapeDtypeStruct((N_PROMPT, D_MODEL), F32), jax.ShapeDtypeStruct((DEC_BATCH, D_MODEL), F32)],
        compiler_params=TOKEN_PARAMS, name="dense_tail",
    )(mix_p, mix_d, x_p, x_d, wo, g1, b1, wg, wu, wd, g2, b2)


TOP_K = 2
TMOE = 512
MOE_TILES = (TOP_K * N_TOK) // TMOE + N_EXPERTS
MOE_ROWS = MOE_TILES * TMOE


def _route(x, rhi_ref, rlo_ref, base, route_ref, gate_ref):
    n = x.shape[0]
    lane_i = lax.broadcasted_iota(jnp.int32, (n, LANES), 1)
    lane = lane_i.astype(F32)
    x_hi = x.astype(BF16)
    x_lo = (x - x_hi.astype(F32)).astype(BF16)
    r_hi = rhi_ref[0]
    logits = (jnp.dot(x_hi, r_hi, preferred_element_type=F32)
              + (jnp.dot(x_lo, r_hi, preferred_element_type=F32)
                 + jnp.dot(x_hi, rlo_ref[0], preferred_element_type=F32)))
    logits = jnp.where(lane_i < N_EXPERTS, logits, -jnp.inf)
    v1 = jnp.max(logits, -1, keepdims=True)
    i1 = jnp.min(jnp.where(logits == v1, lane, float(LANES)), -1, keepdims=True)
    rest = jnp.where(lane == i1, -jnp.inf, logits)
    v2 = jnp.max(rest, -1, keepdims=True)
    i2 = jnp.min(jnp.where(rest == v2, lane, float(LANES)), -1, keepdims=True)
    e2 = jnp.exp(v2 - v1)
    g1 = 1.0 / (1.0 + e2)
    oh1 = jnp.where(lane == i1, 1.0, 0.0)
    oh2 = jnp.where(lane == i2, 1.0, 0.0)
    tr = lax.broadcasted_iota(jnp.int32, (n, n), 0)
    tc = lax.broadcasted_iota(jnp.int32, (n, n), 1)
    before = jnp.where(tc < tr, 1.0, 0.0).astype(BF16)
    cum1 = jnp.dot(before, oh1.astype(BF16), preferred_element_type=F32)
    cum2 = jnp.dot(before, oh2.astype(BF16), preferred_element_type=F32)
    tot1 = jnp.sum(oh1, axis=0, keepdims=True)
    tot2 = jnp.sum(oh2, axis=0, keepdims=True)
    b = base[...]
    rank1 = jnp.sum(oh1 * (cum1 + b), -1, keepdims=True)
    rank2 = jnp.sum(oh2 * (cum2 + b + tot1), -1, keepdims=True)
    base[...] = b + tot1 + tot2
    route = jnp.where(lane_i == 0, i1, jnp.where(lane_i == 1, i2, jnp.where(lane_i == 2, rank1,
                                                                            jnp.where(lane_i == 3, rank2, 0.0))))
    route_ref[0:n, :] = route.astype(jnp.int32)
    gate_ref[0:n, :] = jnp.where(lane_i == 0, g1, jnp.where(lane_i == 1, e2 * g1, 0.0))


def _moe_head_kernel(mixp_ref, mixd_ref, xp_ref, xd_ref, wo_ref, g1_ref, b1_ref, rhi_ref, rlo_ref,
                     wg_ref, wu_ref, wd_ref,
                     x1p_ref, x1d_ref, route_ref, gate_ref, cnt_ref, wgb_ref, wub_ref, wdb_ref, base):
    @pl.when(pl.program_id(0) == 0)
    def _():
        base[...] = jnp.zeros_like(base)

    def head(mix_ref, x_ref, x1_ref):
        x1 = _mixer_residual_norm(mix_ref, x_ref, wo_ref, g1_ref, b1_ref)
        x1_ref[...] = x1
        _route(x1, rhi_ref, rlo_ref, base, route_ref, gate_ref)
        if x_ref is xp_ref:
            wgb_ref[...] = wg_ref[0].astype(BF16)
            wub_ref[...] = wu_ref[0].astype(BF16)
            wdb_ref[...] = wd_ref[0].astype(BF16)

    _on_token_tile(head, (mixp_ref, xp_ref, x1p_ref), (mixd_ref, xd_ref, x1d_ref))
    cnt_ref[...] = jnp.broadcast_to(base[...], cnt_ref.shape).astype(jnp.int32)


def _moe_head(mix_p, mix_d, x_p, x_d, wo, layer, g1, b1, r_hi, r_lo, moe_layer, wg, wu, wd):
    specs = [_cast_specs(w, moe_layer, N_PROMPT_TILES) for w in (wg, wu, wd)]
    return pl.pallas_call(
        _moe_head_kernel,
        grid=(N_TILES,),
        in_specs=[_p_spec(D_MIX), _d_spec(D_MIX), _p_spec(D_MODEL), _d_spec(D_MODEL), _layer_spec(wo, layer),
                  _const_spec(g1), _const_spec(b1), _layer_spec(r_hi, moe_layer), _layer_spec(r_lo, moe_layer)]
                 + [s[0] for s in specs],
        out_specs=[_p_spec(D_MODEL), _d_spec(D_MODEL), _tile_spec(LANES), _tile_spec(LANES),
                   pl.BlockSpec((8, LANES), lambda i: (0, 0))] + [s[1] for s in specs],
        out_shape=[jax.ShapeDtypeStruct((N_PROMPT, D_MODEL), F32), jax.ShapeDtypeStruct((DEC_BATCH, D_MODEL), F32),
                   jax.ShapeDtypeStruct((N_TOK, LANES), jnp.int32), jax.ShapeDtypeStruct((N_TOK, LANES), F32),
                   jax.ShapeDtypeStruct((8, LANES), jnp.int32)]
                  + [jax.ShapeDtypeStruct(w.shape[1:], BF16) for w in (wg, wu, wd)],
        scratch_shapes=[pltpu.VMEM((1, LANES), F32)],
        compiler_params=TOKEN_PARAMS, name="moe_head",
    )(mix_p, mix_d, x_p, x_d, wo, g1, b1, r_hi, r_lo, wg, wu, wd)


def _row_copy(src, src_row, dst, dst_row, sem):
    return pltpu.make_async_copy(src.at[pl.ds(src_row, 1)], dst.at[pl.ds(dst_row, 1)], sem)


DMA_UNROLL = 8


def _dispatch_kernel(pos1_ref, pos2_ref, cnt_ref, off_ref, pad_ref, xp_ref, xd_ref, xs_hbm, sem):
    start = pl.program_id(0) * TM

    def scatter(x_ref):
        n = x_ref.shape[0]

        def issue(r, carry):
            t = start + r
            _row_copy(x_ref, r, xs_hbm, pos1_ref[t], sem).start()
            _row_copy(x_ref, r, xs_hbm, pos2_ref[t], sem).start()
            return carry

        lax.fori_loop(0, n, issue, 0, unroll=DMA_UNROLL)
        for _ in range(TOP_K):
            pltpu.make_async_copy(x_ref, xs_hbm.at[pl.ds(0, n)], sem).wait()

    def scatter_and_pad(x_ref):
        scatter(x_ref)

        def drain(q, carry):
            _row_copy(x_ref, 0, xs_hbm, 0, sem).wait()
            return carry

        for e in range(N_EXPERTS):
            lo = off_ref[e] + cnt_ref[e]

            def fill(q, carry, lo=lo):
                _row_copy(x_ref, 0, xs_hbm, lo + q, sem).start()
                return carry

            lax.fori_loop(0, pad_ref[e], fill, 0)
            lax.fori_loop(0, pad_ref[e], drain, 0)

    _on_token_tile(lambda x_ref, pad: scatter_and_pad(x_ref) if pad else scatter(x_ref),
                   (xp_ref, False), (xd_ref, True))

def _dispatch(x_p, x_d, pos1, pos2, counts, offs, pads):
    return pl.pallas_call(
        _dispatch_kernel,
        grid_spec=pltpu.PrefetchScalarGridSpec(
            num_scalar_prefetch=5, grid=(N_TILES,),
            in_specs=[_p_spec(D_MODEL), _d_spec(D_MODEL)],
            out_specs=pl.BlockSpec(memory_space=pl.ANY),
            scratch_shapes=[pltpu.SemaphoreType.DMA(())]),
        out_shape=jax.ShapeDtypeStruct((MOE_ROWS, D_MODEL), F32),
        compiler_params=TOKEN_PARAMS, name="moe_dispatch",
    )(pos1, pos2, counts, offs, pads, x_p, x_d)


def _gffn_kernel(te_ref, nv_ref, xs_ref, wg_ref, wu_ref, wd_ref, ys_ref):
    @pl.when(pl.program_id(0) < nv_ref[0])
    def _():
        ys_ref[...] = _swiglu(xs_ref[...].astype(BF16), wg_ref[0], wu_ref[0], wd_ref[0])


def _gffn(xs, tile_expert, n_valid, wg, wu, wd):
    def tile(i, te, nv):
        return (jnp.minimum(i, nv[0] - 1), 0)

    def expert(i, te, nv):
        return (te[jnp.minimum(i, nv[0] - 1)], 0, 0)

    return pl.pallas_call(
        _gffn_kernel,
        grid_spec=pltpu.PrefetchScalarGridSpec(
            num_scalar_prefetch=2, grid=(MOE_TILES,),
            in_specs=[pl.BlockSpec((TMOE, D_MODEL), tile), pl.BlockSpec((1, D_MODEL, D_FF), expert),
                      pl.BlockSpec((1, D_MODEL, D_FF), expert), pl.BlockSpec((1, D_FF, D_MODEL), expert)],
            out_specs=pl.BlockSpec((TMOE, D_MODEL), tile)),
        out_shape=jax.ShapeDtypeStruct((MOE_ROWS, D_MODEL), F32),
        compiler_params=pltpu.CompilerParams(dimension_semantics=("arbitrary",), vmem_limit_bytes=VMEM_LIMIT),
        name="moe_ffn",
    )(tile_expert, n_valid, xs, wg, wu, wd)


def _combine_kernel(pos1_ref, pos2_ref, xp_ref, xd_ref, gate_ref, ys_hbm, g_ref, b_ref, op_ref, od_ref, rows, sem):
    start = pl.program_id(0) * TM

    def combine(x_ref, o_ref):
        n = x_ref.shape[0]

        def issue(r, carry):
            t = start + r
            pltpu.make_async_copy(ys_hbm.at[pl.ds(pos1_ref[t], 1)], rows.at[0, pl.ds(r, 1)], sem).start()
            pltpu.make_async_copy(ys_hbm.at[pl.ds(pos2_ref[t], 1)], rows.at[1, pl.ds(r, 1)], sem).start()
            return carry

        lax.fori_loop(0, n, issue, 0, unroll=DMA_UNROLL)
        for s in range(TOP_K):
            pltpu.make_async_copy(ys_hbm.at[pl.ds(0, n)], rows.at[s, pl.ds(0, n)], sem).wait()
        gate = gate_ref[0:n, :]
        f = gate[:, 0:1] * rows[0, 0:n, :] + gate[:, 1:2] * rows[1, 0:n, :]
        o_ref[...] = _layernorm(ALPHA * x_ref[...] + f, g_ref[...], b_ref[...])

    _on_token_tile(combine, (xp_ref, op_ref), (xd_ref, od_ref))


def _combine_ln(x_p, x_d, gates, ys, pos1, pos2, g, b):
    return pl.pallas_call(
        _combine_kernel,
        grid_spec=pltpu.PrefetchScalarGridSpec(
            num_scalar_prefetch=2, grid=(N_TILES,),
            in_specs=[_p_spec(D_MODEL), _d_spec(D_MODEL), _tile_spec(LANES), pl.BlockSpec(memory_space=pl.ANY),
                      _const_spec(g), _const_spec(b)],
            out_specs=[_p_spec(D_MODEL), _d_spec(D_MODEL)],
            scratch_shapes=[pltpu.VMEM((TOP_K, TM, D_MODEL), F32), pltpu.SemaphoreType.DMA(())]),
        out_shape=[jax.ShapeDtypeStruct((N_PROMPT, D_MODEL), F32), jax.ShapeDtypeStruct((DEC_BATCH, D_MODEL), F32)],
        compiler_params=TOKEN_PARAMS, name="moe_combine_ln",
    )(pos1, pos2, x_p, x_d, gates, ys, g, b)


def _moe_tail(mix_p, mix_d, x_p, x_d, wo, layer, g1, b1, r_hi, r_lo, moe_layer, wg, wu, wd, g2, b2):
    flat = [w.reshape(w.shape[0], -1, w.shape[-1]) for w in (wg, wu, wd)]
    x1_p, x1_d, route, gates, cnt, wg_b, wu_b, wd_b = _moe_head(mix_p, mix_d, x_p, x_d, wo, layer, g1, b1,
                                                                r_hi, r_lo, moe_layer, *flat)
    wg_b, wu_b, wd_b = (o.reshape(w.shape[1:]) for o, w in zip((wg_b, wu_b, wd_b), (wg, wu, wd)))
    counts = cnt[0, :N_EXPERTS]
    padded = (counts + TMOE - 1) // TMOE * TMOE
    ends = jnp.cumsum(padded)
    offs = ends - padded
    pos1 = jnp.take(offs, route[:, 0]) + route[:, 2]
    pos2 = jnp.take(offs, route[:, 1]) + route[:, 3]
    tile_ends = ends // TMOE
    tile_ids = jnp.arange(MOE_TILES, dtype=jnp.int32)
    tile_expert = jnp.minimum(jnp.sum(tile_ids[:, None] >= tile_ends[None, :], axis=1), N_EXPERTS - 1)
    xs = _dispatch(x1_p, x1_d, pos1, pos2, counts, offs, padded - counts)
    ys = _gffn(xs, tile_expert.astype(jnp.int32), tile_ends[N_EXPERTS - 1:], wg_b, wu_b, wd_b)
    return _combine_ln(x1_p, x1_d, gates, ys, pos1, pos2, g2, b2)


def _rope_tables(pos):
    half = HEAD_DIM // 2
    inv_freq = ROPE_THETA ** (-jnp.arange(half, dtype=F32) / half)
    ang = pos.astype(F32)[:, None] * inv_freq[None, :]
    cos = jnp.cos(ang)
    sin = jnp.sin(ang)
    return jnp.concatenate([cos] * 4, axis=1), jnp.concatenate([-sin, sin, -sin, sin], axis=1)


def _pad_lanes(a):
    return jnp.pad(a, [(0, 0)] * (a.ndim - 1) + [(0, LANES - a.shape[-1])])


def kernel(x_prompt, x_sample, cache_win_k, cache_win_v, state_ssd, state_ssd_conv, state_conv, w_in, attn_sinks,
           ssd_conv_w, ssd_conv_b, ssd_dt_bias, ssd_a_log, ssd_d, ssd_norm_w, sconv_w, w_out, ln1_g, ln1_b, ln2_g,
           ln2_b, ffn_w_gate, ffn_w_up, ffn_w_down, moe_router, moe_w_gate, moe_w_up, moe_w_down):
    x_p = x_prompt.reshape(N_PROMPT, D_MODEL)
    x_d = x_sample.reshape(DEC_BATCH, D_MODEL)
    cos_p, sin_p = _rope_tables(jnp.arange(SEQ, dtype=jnp.int32))
    cos_s, sin_s = _rope_tables(PAST_LEN + jnp.arange(1, dtype=jnp.int32))
    split = C_GB + SSD_HEADS
    w_in_r = jnp.concatenate([w_in[:, :, :C_GB], w_in[:, :, split:], _pad_lanes(w_in[:, :, C_GB:split])],
                             axis=-1).astype(BF16)
    w_out_b = w_out.astype(BF16)
    dtb = _pad_lanes(ssd_dt_bias)[:, None, :]
    alog = _pad_lanes(ssd_a_log)[:, None, :]
    dskip = jnp.repeat(ssd_d, SSD_HEAD_DIM, axis=-1)[:, None, :]
    router = _pad_lanes(moe_router)
    r_hi = router.astype(BF16)
    r_lo = (router - r_hi.astype(F32)).astype(BF16)
    sink_rows = jnp.broadcast_to(attn_sinks[:, :, None], (DEPTH, N_HEADS, LANES))
    kc = cache_win_k.reshape(DEPTH, DEC_BATCH, WINDOW, D_KV)
    vc = cache_win_v.reshape(DEPTH, DEC_BATCH, WINDOW, D_KV)
    st = state_ssd.reshape(DEPTH, DEC_BATCH, SSD_HEADS * D_STATE_FLAT)
    sbuf = state_ssd_conv.reshape(DEPTH, DEC_BATCH, (SSD_CONV - 1) * D_XBC)
    cbuf = state_conv.reshape(DEPTH, DEC_BATCH, (CONV_WIDTH - 1) * D_CONV)

    outs = {n: [] for n in ("pk", "pv", "ph", "psc", "pc", "sk", "sv", "sh", "ssc", "sc")}
    for i in range(DEPTH):
        j = i // 2
        dense = i % 2 == 0
        if dense:
            proj_p, proj_d, wg_b, wu_b, wd_b = _inproj(x_p, x_d, w_in_r, i, (ffn_w_gate, ffn_w_up, ffn_w_down), j)
        else:
            proj_p, proj_d = _inproj(x_p, x_d, w_in_r, i)
        proj_p = proj_p.reshape(BATCH, SEQ, D_PROJ)
        params = (ssd_conv_w[i], ssd_conv_b[i][None, :], dtb[i], alog[i], dskip[i], ssd_norm_w[i][None, :],
                  sconv_w[i])
        mix_p, klast, hlast, culast = _mixer_prompt(proj_p, attn_sinks[i], cos_p, sin_p, *params)
        oattn, knew = _dec_attn(proj_d, kc, vc, i, cos_s, sin_s, sink_rows[i])
        mix_d, stnew, sbufnew, cbufnew = _dec_ssd(oattn, proj_d, sbuf, cbuf, st, i, *params)
        mix_p = mix_p.reshape(N_PROMPT, D_MIX)
        ln1 = (ln1_g[i][None, :], ln1_b[i][None, :])
        ln2 = (ln2_g[i][None, :], ln2_b[i][None, :])
        if dense:
            x_p, x_d = _dense_tail(mix_p, mix_d, x_p, x_d, w_out_b, i, *ln1, wg_b, wu_b, wd_b, *ln2)
        else:
            x_p, x_d = _moe_tail(mix_p, mix_d, x_p, x_d, w_out_b, i, *ln1, r_hi, r_lo, j,
                                 moe_w_gate, moe_w_up, moe_w_down, *ln2)

        def tail(n_rows, col, width):
            return proj_p[:, SEQ - n_rows:, col:col + width]

        outs["pk"].append(klast.reshape(BATCH, WINDOW, N_KV_HEADS, HEAD_DIM))
        outs["pv"].append(tail(WINDOW, C_V, D_KV).reshape(BATCH, WINDOW, N_KV_HEADS, HEAD_DIM))
        outs["ph"].append(hlast)
        outs["psc"].append(tail(SSD_CONV - 1, C_XBC, D_XBC))
        outs["pc"].append(culast[:, 8 - (CONV_WIDTH - 1):, :])
        outs["sk"].append(knew)
        outs["sv"].append(proj_d[:, C_V:C_V + D_KV])
        outs["sh"].append(stnew.reshape(DEC_BATCH, SSD_HEADS, SSD_HEAD_DIM, SSD_STATE))
        outs["ssc"].append(sbufnew.reshape(DEC_BATCH, SSD_CONV - 1, D_XBC))
        outs["sc"].append(cbufnew.reshape(DEC_BATCH, CONV_WIDTH - 1, D_CONV))
    y_prompt = x_p.reshape(BATCH, SEQ, D_MODEL)
    y_sample = x_d.reshape(DEC_BATCH, 1, D_MODEL)
    stk = lambda n: jnp.stack(outs[n])

    def slide(cache, new):
        win = jnp.concatenate([cache[:, :, 1:], stk(new)[:, :, None, :]], axis=2)
        return win.reshape(DEPTH, DEC_BATCH, WINDOW, N_KV_HEADS, HEAD_DIM)

    return (y_prompt, y_sample, stk("pk"), stk("pv"), stk("ph"), stk("psc"), stk("pc"),
            slide(kc, "sk"), slide(vc, "sv"),stk("sh"), stk("ssc"), stk("sc"))
```

```python
import functools
import math

import jax
import jax.numpy as jnp
from jax import lax
from jax.experimental import pallas as pl
from jax.experimental.pallas import tpu as pltpu

D_MODEL = 1024
BATCH = 2
SEQ = 8192
DEPTH = 4
DEC_BATCH = 128
PAST_LEN = 8192
N_HEADS = 8
N_KV_HEADS = 2
HEAD_DIM = 64
Q_PER_KV = N_HEADS // N_KV_HEADS
D_ATTN = N_HEADS * HEAD_DIM
D_KV = N_KV_HEADS * HEAD_DIM
WINDOW = 128
BLK = 128
ROPE_THETA = 10000.0
SSD_HEADS = 4
SSD_HEAD_DIM = 64
D_SSD = SSD_HEADS * SSD_HEAD_DIM
SSD_GROUPS = 2
SSD_STATE = 64
SSD_CONV = 4
D_XBC = D_SSD + 2 * SSD_GROUPS * SSD_STATE
D_CONV = 256
CONV_WIDTH = 3
D_MIX = D_ATTN + D_SSD + D_CONV
D_FF = 2816
N_EXPERTS = 8
ALPHA = (2 * DEPTH) ** 0.25
NORM_EPS = 1e-5

LANES = 128
N_PROMPT = BATCH * SEQ
N_TOK = N_PROMPT + DEC_BATCH
N_BLK = SEQ // BLK
TM = 512
N_TILES = pl.cdiv(N_TOK, TM)
VMEM_LIMIT = 56 * 1024 * 1024

C_Q, C_K, C_V, C_Z, C_XBC = 0, 512, 640, 768, 1024
C_GB, C_GC, C_U, C_DT = 1536, 1792, 2048, 2304
D_PROJ = C_DT + LANES

BF16 = jnp.bfloat16
F32 = jnp.float32


def _sigmoid(x):
    return 1.0 / (1.0 + jnp.exp(-x))


def _silu(x):
    return x * _sigmoid(x)


def _softplus(x):
    return jnp.maximum(x, 0.0) + jnp.log1p(jnp.exp(-jnp.abs(x)))


def _layernorm(xf, g, b):
    mu = jnp.mean(xf, -1, keepdims=True)
    xc = xf - mu
    var = jnp.mean(xc * xc, -1, keepdims=True)
    return xc * lax.rsqrt(var + NORM_EPS) * g + b


def _rope(x, cos, sin):
    w = x.shape[1]
    reps = w // LANES
    if reps > 1:
        cos = jnp.concatenate([cos] * reps, axis=1)
        sin = jnp.concatenate([sin] * reps, axis=1)
    lane = lax.broadcasted_iota(jnp.int32, x.shape, 1)
    first_half = (lane % HEAD_DIM) < (HEAD_DIM // 2)
    partner = jnp.where(first_half, pltpu.roll(x, w - HEAD_DIM // 2, 1), pltpu.roll(x, HEAD_DIM // 2, 1))
    return x * cos + partner * sin


def _dot(a, b):
    return jnp.dot(a.astype(BF16), b.astype(BF16), preferred_element_type=F32)


def _dot_nt(a, b):
    return lax.dot_general(a.astype(BF16), b.astype(BF16), (((1,), (1,)), ((), ())), preferred_element_type=F32)


def _dot_tn(a, b):
    return lax.dot_general(a.astype(BF16), b.astype(BF16), (((0,), (0,)), ((), ())), preferred_element_type=F32)


N_PROMPT_TILES = N_PROMPT // TM
TOKEN_PARAMS = pltpu.CompilerParams(dimension_semantics=("arbitrary",), vmem_limit_bytes=VMEM_LIMIT)
WHOLE_VMEM = pl.BlockSpec(memory_space=pltpu.VMEM)


def _p_spec(width):
    return pl.BlockSpec((TM, width), lambda i, *_: (jnp.minimum(i, N_PROMPT_TILES - 1), 0))


def _d_spec(width):
    return pl.BlockSpec((DEC_BATCH, width), lambda i, *_: (0, 0))


def _tile_spec(width):
    return pl.BlockSpec((TM, width), lambda i, *_: (i, 0))


def _const_spec(a):
    return pl.BlockSpec(a.shape, lambda i, *_: (0,) * a.ndim)


def _layer_spec(a, layer):
    return pl.BlockSpec((1,) + a.shape[1:], lambda i, *_: (layer,) + (0,) * (a.ndim - 1))


def _on_token_tile(fn, prompt_refs, decode_refs):
    i = pl.program_id(0)

    @pl.when(i < N_PROMPT_TILES)
    def _():
        fn(*prompt_refs)

    @pl.when(i == N_PROMPT_TILES)
    def _():
        fn(*decode_refs)


def _cast_specs(w, layer, steps):
    _, rows, cols = w.shape
    return (pl.BlockSpec((1, rows // steps, cols), lambda i, *_: (layer, jnp.minimum(i, steps - 1), 0)),
            pl.BlockSpec((rows // steps, cols), lambda i, *_: (jnp.minimum(i, steps - 1), 0)))


FFN_CAST_STEPS = 16


def _inproj_kernel(xp_ref, xd_ref, w_ref, *refs):
    if len(refs) == 2:
        pp_ref, pd_ref = refs
    else:
        wg_ref, wu_ref, wd_ref, pp_ref, pd_ref, wgb_ref, wub_ref, wdb_ref = refs

    def project(x_ref, o_ref):
        o_ref[...] = jnp.dot(x_ref[...].astype(BF16), w_ref[0], preferred_element_type=F32)

    _on_token_tile(project, (xp_ref, pp_ref), (xd_ref, pd_ref))
    if len(refs) > 2:
        @pl.when(pl.program_id(0) < FFN_CAST_STEPS)
        def _():
            wgb_ref[...] = wg_ref[0].astype(BF16)
            wub_ref[...] = wu_ref[0].astype(BF16)
            wdb_ref[...] = wd_ref[0].astype(BF16)


def _inproj(x_p, x_d, w, layer, ffn_w=None, ffn_layer=0):
    in_specs = [_p_spec(D_MODEL), _d_spec(D_MODEL), _layer_spec(w, layer)]
    out_specs = [_p_spec(D_PROJ), _d_spec(D_PROJ)]
    out_shape = [jax.ShapeDtypeStruct((N_PROMPT, D_PROJ), F32), jax.ShapeDtypeStruct((DEC_BATCH, D_PROJ), F32)]
    args = [x_p, x_d, w]
    if ffn_w is not None:
        specs = [_cast_specs(a, ffn_layer, FFN_CAST_STEPS) for a in ffn_w]
        in_specs += [s[0] for s in specs]
        out_specs += [s[1] for s in specs]
        out_shape += [jax.ShapeDtypeStruct(a.shape[1:], BF16) for a in ffn_w]
        args += list(ffn_w)
    return pl.pallas_call(
        _inproj_kernel, grid=(N_TILES,), in_specs=in_specs, out_specs=out_specs, out_shape=out_shape,
        compiler_params=TOKEN_PARAMS, name="inproj",
    )(*args)


def _ssd_gate_norm(y, z, norm_w):
    y = y * _silu(z)
    return y * lax.rsqrt(jnp.mean(y * y, -1, keepdims=True) + NORM_EPS) * norm_w


def _attention_block(b, sinks_ref, q, k, v, bias, kbd, vbd, mix_ref):
    lo = lax.broadcasted_iota(jnp.int32, (BLK, LANES), 1) < HEAD_DIM
    k_sw = pltpu.roll(k, HEAD_DIM, 1)
    v_sw = pltpu.roll(v, HEAD_DIM, 1)
    for g in range(N_KV_HEADS):
        for ref, x, x_sw in ((kbd, k, k_sw), (vbd, v, v_sw)):
            a_src, b_src = (x, x_sw) if g == 0 else (x_sw, x)
            ref[b, g, 0:BLK, :] = ref[b, g, BLK:2 * BLK, :]
            ref[b, g, 2 * BLK:3 * BLK, :] = ref[b, g, 3 * BLK:4 * BLK, :]
            ref[b, g, BLK:2 * BLK, :] = jnp.where(lo, a_src, 0.0).astype(BF16)
            ref[b, g, 3 * BLK:4 * BLK, :] = jnp.where(lo, 0.0, b_src).astype(BF16)
    for m in range(N_HEADS // 2):
        g = (2 * m) // Q_PER_KV
        s = _dot_nt(q[:, m * LANES:(m + 1) * LANES], kbd[b, g]) + bias
        probs, inv = [], []
        for hh in range(2):
            sh = s[:, hh * 2 * BLK:(hh + 1) * 2 * BLK]
            sink = sinks_ref[2 * m + hh]
            mx = jnp.maximum(jnp.max(sh, -1, keepdims=True), sink)
            p = jnp.exp(sh - mx)
            inv.append(1.0 / (jnp.sum(p, -1, keepdims=True) + jnp.exp(sink - mx)))
            probs.append(p.astype(BF16))
        o = jnp.dot(jnp.concatenate(probs, axis=1), vbd[b, g], preferred_element_type=F32)
        mix_ref[b, :, m * LANES:(m + 1) * LANES] = (o * jnp.where(lo, inv[0], inv[1])).astype(mix_ref.dtype)


def _mixer_prompt_kernel(sinks_ref, proj_ref,
                         cos_ref, sin_ref, bias_ref, cw_ref, cb_ref, dtb_ref, alog_ref, dskip_ref, nw_ref, scw_ref,
                         mix_ref, klast_ref, hlast_ref, culast_ref,
                         kbd, vbd, conv_ext, cu_ext, hstate):
    j = pl.program_id(0)

    def cols(start, width):
        return proj_ref.at[:, :, pl.ds(start, width)]

    q_ref, k_ref, v_ref, z_ref = cols(C_Q, D_ATTN), cols(C_K, D_KV), cols(C_V, D_KV), cols(C_Z, D_SSD)
    xbc_ref, dt_ref = cols(C_XBC, D_XBC), cols(C_DT, LANES)
    gb_ref, gc_ref, u_ref = cols(C_GB, D_CONV), cols(C_GC, D_CONV), cols(C_U, D_CONV)

    @pl.when(j == 0)
    def _():
        kbd[...] = jnp.zeros_like(kbd)
        vbd[...] = jnp.zeros_like(vbd)
        conv_ext[:, 0:8, :] = jnp.zeros((BATCH, 8, D_XBC), F32)
        cu_ext[:, 0:8, :] = jnp.zeros((BATCH, 8, D_CONV), F32)
        hstate[...] = jnp.zeros_like(hstate)

    cos = cos_ref[...]
    sin = sin_ref[...]
    bias = bias_ref[jnp.minimum(j, 1)]
    for b in range(BATCH):
        q = _rope(q_ref[b], cos, sin) * (1.0 / math.sqrt(HEAD_DIM))
        k = _rope(k_ref[b], cos, sin)
        klast_ref[b] = k
        _attention_block(b, sinks_ref, q, k, v_ref[b], bias, kbd, vbd, mix_ref)
        _ssd_block(b, z_ref, xbc_ref, dt_ref, cw_ref, cb_ref, dtb_ref, alog_ref, dskip_ref, nw_ref,
                   mix_ref, hlast_ref, conv_ext, hstate)
        _conv_block(b, gb_ref, gc_ref, u_ref, scw_ref, mix_ref, culast_ref, cu_ext)


def _ssd_block(b, z_ref, xbc_ref, dt_ref, cw_ref, cb_ref, dtb_ref, alog_ref, dskip_ref, nw_ref,
               mix_ref, hlast_ref, conv_ext, hstate):
    conv_ext[b, 8:8 + BLK, :] = xbc_ref[b]
    acc = conv_ext[b, 5:5 + BLK, :] * cw_ref[0:1, :]
    for t in range(1, SSD_CONV):
        acc = acc + conv_ext[b, 5 + t:5 + t + BLK, :] * cw_ref[t:t + 1, :]
    conv_ext[b, 0:8, :] = conv_ext[b, BLK:BLK + 8, :]
    xbc = _silu(acc + cb_ref[...])
    xs = xbc[:, 0:D_SSD]
    bm = xbc[:, D_SSD:D_SSD + LANES]
    cm = xbc[:, D_SSD + LANES:D_SSD + 2 * LANES]
    dt = _softplus(dt_ref[b] + dtb_ref[...])
    dta = dt * (-jnp.exp(alog_ref[...]))
    ti = lax.broadcasted_iota(jnp.int32, (BLK, BLK), 0)
    si = lax.broadcasted_iota(jnp.int32, (BLK, BLK), 1)
    causal = si <= ti
    cum = jnp.dot(causal.astype(F32), dta, preferred_element_type=F32, precision=lax.Precision.HIGHEST)
    cum_t = cum.T
    ys = []
    for g in range(SSD_GROUPS):
        sl = slice(g * SSD_STATE, (g + 1) * SSD_STATE)
        b_g = bm[:, sl]
        c_g = cm[:, sl]
        cb_scores = _dot_nt(c_g, b_g)
        for hh in range(SSD_HEADS // SSD_GROUPS):
            h = g * (SSD_HEADS // SSD_GROUPS) + hh
            ccol = cum[:, h:h + 1]
            crow = cum_t[h:h + 1, :]
            clast = cum[BLK - 1:BLK, h:h + 1]
            decay = jnp.exp(jnp.where(causal, ccol - crow, -jnp.inf))
            x_h = xs[:, h * SSD_HEAD_DIM:(h + 1) * SSD_HEAD_DIM]
            xdt = x_h * dt[:, h:h + 1]
            h_prev = hstate[b, h]
            y = _dot(cb_scores * decay, xdt)
            y = y + _dot_nt(c_g, h_prev) * jnp.exp(ccol)
            states = _dot_tn(xdt * jnp.exp(clast - ccol), b_g)
            hstate[b, h] = h_prev * jnp.exp(clast) + states
            ys.append(y)
    y = jnp.concatenate(ys, axis=1) + dskip_ref[...] * xs
    mix_ref[b, :, D_ATTN:D_ATTN + D_SSD] = _ssd_gate_norm(y, z_ref[b], nw_ref[...]).astype(mix_ref.dtype)
    hlast_ref[b] = hstate[b]


def _conv_block(b, gb_ref, gc_ref, u_ref, scw_ref, mix_ref, culast_ref, cu_ext):
    cu_ext[b, 8:8 + BLK, :] = gc_ref[b] * u_ref[b]
    cc = cu_ext[b, 6:6 + BLK, :] * scw_ref[0:1, :]
    for t in range(1, CONV_WIDTH):
        cc = cc + cu_ext[b, 6 + t:6 + t + BLK, :] * scw_ref[t:t + 1, :]
    tail = cu_ext[b, BLK:BLK + 8, :]
    cu_ext[b, 0:8, :] = tail
    culast_ref[b] = tail
    mix_ref[b, :, D_ATTN + D_SSD:D_MIX] = (gb_ref[b] * cc).astype(mix_ref.dtype)


def _attn_bias():
    r = jnp.arange(BLK, dtype=jnp.int32)[:, None]
    c = jnp.arange(4 * BLK, dtype=jnp.int32)[None, :] % (2 * BLK)
    window = (c >= r) & (c - BLK <= r)
    first = window & (c >= BLK)
    return jnp.where(jnp.stack([first, window]), 0.0, -jnp.inf).astype(F32)


def _mixer_prompt(proj, sinks, cos_tab, sin_tab, cw, cb, dtb, alog, dskip, nw, scw):
    def full(a):
        return pl.BlockSpec(a.shape, lambda j, *_: (0,) * a.ndim)

    bias = _attn_bias()
    params = (cw, cb, dtb, alog, dskip, nw, scw)
    out_shape = [jax.ShapeDtypeStruct((BATCH, SEQ, D_MIX), BF16),
                 jax.ShapeDtypeStruct((BATCH, BLK, D_KV), F32),
                 jax.ShapeDtypeStruct((BATCH, SSD_HEADS, SSD_HEAD_DIM, SSD_STATE), F32),
                 jax.ShapeDtypeStruct((BATCH, 8, D_CONV), F32)]
    grid_spec = pltpu.PrefetchScalarGridSpec(
        num_scalar_prefetch=1,
        grid=(N_BLK,),
        in_specs=[pl.BlockSpec((BATCH, BLK, D_PROJ), lambda j, *_: (0, j, 0)),
                  pl.BlockSpec((BLK, LANES), lambda j, *_: (j, 0)),
                  pl.BlockSpec((BLK, LANES), lambda j, *_: (j, 0)), full(bias)] + [full(a) for a in params],
        out_specs=[pl.BlockSpec((BATCH, BLK, D_MIX), lambda j, *_: (0, j, 0))] + [full(o) for o in out_shape[1:]],
        scratch_shapes=[pltpu.VMEM((BATCH, N_KV_HEADS, 4 * BLK, LANES), BF16),
                        pltpu.VMEM((BATCH, N_KV_HEADS, 4 * BLK, LANES), BF16),
                        pltpu.VMEM((BATCH, BLK + 8, D_XBC), F32), pltpu.VMEM((BATCH, BLK + 8, D_CONV), F32),
                        pltpu.VMEM((BATCH, SSD_HEADS, SSD_HEAD_DIM, SSD_STATE), F32)],
    )
    return pl.pallas_call(
        _mixer_prompt_kernel,
        grid_spec=grid_spec,
        out_shape=out_shape,
        compiler_params=pltpu.CompilerParams(dimension_semantics=("arbitrary",), vmem_limit_bytes=VMEM_LIMIT),
        name="mixer_prompt",
    )(sinks, proj, cos_tab, sin_tab, bias, *params)


DEC_BB = 32


def _dec_attn_kernel(q_ref, k_ref, v_ref, kc_ref, vc_ref, cos_ref, sin_ref, sink_ref,
                     o_ref, knew_ref, qbd, kn8, vn8, o_scr):
    cos = cos_ref[...]
    sin = sin_ref[...]
    q = _rope(q_ref[...], cos, sin) * (1.0 / math.sqrt(HEAD_DIM))
    kn = _rope(k_ref[...], cos, sin)
    vn = v_ref[...]
    knew_ref[...] = kn
    lane = lax.broadcasted_iota(jnp.int32, (DEC_BB, LANES), 1)
    for h in range(N_HEADS):
        g = h // Q_PER_KV
        pair = q[:, (h // 2) * LANES:(h // 2 + 1) * LANES]
        if h % 2 != g:
            pair = pltpu.roll(pair, HEAD_DIM, 1)
        qbd[pl.ds(h, DEC_BB, stride=8), :] = jnp.where(lane // HEAD_DIM == g, pair, 0.0)
        kn8[pl.ds(h, DEC_BB, stride=8), :] = kn
        vn8[pl.ds(h, DEC_BB, stride=8), :] = vn
    q3 = qbd[...].reshape(DEC_BB, N_HEADS, LANES)
    kn3 = kn8[...].reshape(DEC_BB, N_HEADS, LANES)
    vn3 = vn8[...].reshape(DEC_BB, N_HEADS, LANES)
    s3 = jnp.einsum('bhl,bsl->bhs', q3.astype(BF16), kc_ref[0].astype(BF16), preferred_element_type=F32)
    s_self = jnp.sum(q3 * kn3, axis=-1, keepdims=True)
    sink3 = sink_ref[...][None, :, 0:1]
    m = jnp.maximum(jnp.maximum(jnp.max(s3, -1, keepdims=True), s_self), sink3)
    p3 = jnp.exp(s3 - m)
    p_self = jnp.exp(s_self - m)
    denom = jnp.sum(p3, -1, keepdims=True) + p_self + jnp.exp(sink3 - m)
    o3 = jnp.einsum('bhs,bsl->bhl', p3.astype(BF16), vc_ref[0].astype(BF16), preferred_element_type=F32)
    o3 = (o3 + p_self * vn3) * (1.0 / denom)
    h3 = lax.broadcasted_iota(jnp.int32, o3.shape, 1)
    l3 = lax.broadcasted_iota(jnp.int32, o3.shape, 2)
    o3 = jnp.where(l3 // HEAD_DIM == h3 // Q_PER_KV, o3, 0.0)
    o_scr[...] = o3.reshape(DEC_BB * N_HEADS, LANES)
    for pair in range(N_HEADS // 2):
        g = (2 * pair) // Q_PER_KV
        a = o_scr[pl.ds(2 * pair, DEC_BB, stride=8), :]
        b = o_scr[pl.ds(2 * pair + 1, DEC_BB, stride=8), :]
        if g == 1:
            a = pltpu.roll(a, HEAD_DIM, 1)
        else:
            b = pltpu.roll(b, HEAD_DIM, 1)
        o_ref[:, pair * LANES:(pair + 1) * LANES] = a + b


def _dec_attn(proj, kc, vc, layer, cos1, sin1, sink_rows):
    def pblk(width, col):
        return pl.BlockSpec((DEC_BB, width), lambda i: (i, col // width))

    def full(a):
        return pl.BlockSpec(a.shape, lambda i: (0,) * a.ndim)

    return pl.pallas_call(
        _dec_attn_kernel,
        grid=(DEC_BATCH // DEC_BB,),
        in_specs=[pblk(D_ATTN, C_Q), pblk(D_KV, C_K), pblk(D_KV, C_V),
                  pl.BlockSpec((1, DEC_BB, WINDOW, D_KV), lambda i: (layer, i, 0, 0)),
                  pl.BlockSpec((1, DEC_BB, WINDOW, D_KV), lambda i: (layer, i, 0, 0)),
                  full(cos1), full(sin1), full(sink_rows)],
        out_specs=[pl.BlockSpec((DEC_BB, D_ATTN), lambda i: (i, 0)),
                   pl.BlockSpec((DEC_BB, D_KV), lambda i: (i, 0))],
        out_shape=[jax.ShapeDtypeStruct((DEC_BATCH, D_ATTN), F32),
                   jax.ShapeDtypeStruct((DEC_BATCH, D_KV), F32)],
        scratch_shapes=[pltpu.VMEM((DEC_BB * N_HEADS, LANES), F32)] * 4,
        compiler_params=pltpu.CompilerParams(dimension_semantics=("arbitrary",), vmem_limit_bytes=VMEM_LIMIT),
        name="dec_attn",
    )(proj, proj, proj, kc, vc, cos1, sin1, sink_rows)


D_STATE_FLAT = SSD_HEAD_DIM * SSD_STATE


def _dec_ssd_kernel(oattn_ref, z_ref, xbc_ref, gb_ref, gc_ref, u_ref, dt_ref, sbuf_ref, cbuf_ref, st_ref,
                    cw_ref, cb_ref, dtb_ref, alog_ref, dskip_ref, nw_ref, scw_ref,
                    mix_ref, stnew_ref, sbufnew_ref, cbufnew_ref,
                    st_t, stnew_t, xdt_t, b_t, c_t, dec_t, y_t):
    xbc_new = xbc_ref[...]
    acc = sbuf_ref[0, :,0:D_XBC] * cw_ref[0:1, :]
    for t in range(1, SSD_CONV - 1):
        acc = acc + sbuf_ref[0, :,t * D_XBC:(t + 1) * D_XBC] * cw_ref[t:t + 1, :]
    acc = acc + xbc_new * cw_ref[SSD_CONV - 1:SSD_CONV, :]
    sbufnew_ref[:, 0:(SSD_CONV - 2) * D_XBC] = sbuf_ref[0, :,D_XBC:(SSD_CONV - 1) * D_XBC]
    sbufnew_ref[:, (SSD_CONV - 2) * D_XBC:(SSD_CONV - 1) * D_XBC] = xbc_new
    xbc = _silu(acc + cb_ref[...])
    xs = xbc[:, 0:D_SSD]
    dt = _softplus(dt_ref[...] + dtb_ref[...])
    dta = dt * (-jnp.exp(alog_ref[...]))
    dec_t[...] = jnp.exp(dta).T
    dt_t = dt.T
    xs_t = xs.T
    for h in range(SSD_HEADS):
        sl = slice(h * SSD_HEAD_DIM, (h + 1) * SSD_HEAD_DIM)
        xdt_t[sl, :] = xs_t[sl, :] * dt_t[h:h + 1, :]
    b_t[...] = xbc[:, D_SSD:D_SSD + LANES].T
    c_t[...] = xbc[:, D_SSD + LANES:D_SSD + 2 * LANES].T
    for h in range(SSD_HEADS):
        g = h // (SSD_HEADS // SSD_GROUPS)
        st_t[...] = st_ref[0, :, h * D_STATE_FLAT:(h + 1) * D_STATE_FLAT].T
        dec = dec_t[h:h + 1, :]
        bg = b_t[g * SSD_STATE:(g + 1) * SSD_STATE, :]
        cg = c_t[g * SSD_STATE:(g + 1) * SSD_STATE, :]

        def body(p, carry, h=h, dec=dec, bg=bg, cg=cg):
            off = pl.multiple_of(p * SSD_STATE, SSD_STATE)
            new = st_t[pl.ds(off, SSD_STATE), :] * dec + xdt_t[pl.ds(h * SSD_HEAD_DIM + p, 1), :] * bg
            stnew_t[pl.ds(off, SSD_STATE), :] = new
            y_t[pl.ds(h * SSD_HEAD_DIM + p, 1), :] = jnp.sum(new * cg, axis=0, keepdims=True)
            return carry

        lax.fori_loop(0, SSD_HEAD_DIM, body, 0)
        stnew_ref[:, h * D_STATE_FLAT:(h + 1) * D_STATE_FLAT] = stnew_t[...].T
    y = y_t[...].T + dskip_ref[...] * xs
    mix_ref[:, 0:D_ATTN] = oattn_ref[...].astype(mix_ref.dtype)
    mix_ref[:, D_ATTN:D_ATTN + D_SSD] = _ssd_gate_norm(y, z_ref[...], nw_ref[...]).astype(mix_ref.dtype)

    cu = gc_ref[...] * u_ref[...]
    cc = cbuf_ref[0, :,0:D_CONV] * scw_ref[0:1, :] + cbuf_ref[0, :,D_CONV:2 * D_CONV] * scw_ref[1:2, :]
    cc = cc + cu * scw_ref[2:3, :]
    cbufnew_ref[:, 0:D_CONV] = cbuf_ref[0, :,D_CONV:2 * D_CONV]
    cbufnew_ref[:, D_CONV:2 * D_CONV] = cu
    mix_ref[:, D_ATTN + D_SSD:D_MIX] = (gb_ref[...] * cc).astype(mix_ref.dtype)


def _dec_ssd(oattn, proj, sbuf, cbuf, st, layer, cw, cb, dtb, alog, dskip, nw, scw):
    def pblk(width, col):
        return pl.BlockSpec((DEC_BATCH, width), lambda i: (0, col // width))

    def full(a):
        return pl.BlockSpec(a.shape, lambda i: (0,) * a.ndim)

    def new(a):
        return pl.BlockSpec(a.shape[1:], lambda i: (0, 0))

    params = (cw, cb, dtb, alog, dskip, nw, scw)
    return pl.pallas_call(
        _dec_ssd_kernel,
        grid=(1,),
        in_specs=[full(oattn), pblk(D_SSD, C_Z), pblk(D_XBC, C_XBC), pblk(D_CONV, C_GB), pblk(D_CONV, C_GC),
                  pblk(D_CONV, C_U), pblk(LANES, C_DT), _layer_spec(sbuf, layer), _layer_spec(cbuf, layer),
                  _layer_spec(st, layer)] + [full(a) for a in params],
        out_specs=[pl.BlockSpec((DEC_BATCH, D_MIX), lambda i: (0, 0)), new(st), new(sbuf), new(cbuf)],
        out_shape=[jax.ShapeDtypeStruct((DEC_BATCH, D_MIX), BF16), jax.ShapeDtypeStruct(st.shape[1:], F32),
                   jax.ShapeDtypeStruct(sbuf.shape[1:], F32), jax.ShapeDtypeStruct(cbuf.shape[1:], F32)],
        scratch_shapes=[pltpu.VMEM((D_STATE_FLAT, DEC_BATCH), F32), pltpu.VMEM((D_STATE_FLAT, DEC_BATCH), F32),
                        pltpu.VMEM((D_SSD, DEC_BATCH), F32), pltpu.VMEM((LANES, DEC_BATCH), F32),
                        pltpu.VMEM((LANES, DEC_BATCH), F32), pltpu.VMEM((LANES, DEC_BATCH), F32),
                        pltpu.VMEM((D_SSD, DEC_BATCH), F32)],
        compiler_params=pltpu.CompilerParams(dimension_semantics=("arbitrary",), vmem_limit_bytes=VMEM_LIMIT),
        name="dec_ssd",
    )(oattn, proj, proj, proj, proj, proj, proj, sbuf, cbuf, st, *params)


def _swiglu(xb, wg, wu, wd):
    gate = jnp.dot(xb, wg, preferred_element_type=F32)
    up = jnp.dot(xb, wu, preferred_element_type=F32)
    return jnp.dot((_silu(gate) * up).astype(BF16), wd, preferred_element_type=F32)


def _mixer_residual_norm(mix_ref, x_ref, wo_ref, g_ref, b_ref):
    m = jnp.dot(mix_ref[...], wo_ref[0], preferred_element_type=F32)
    return _layernorm(ALPHA * x_ref[...] + m, g_ref[...], b_ref[...])


def _dense_tail_kernel(mixp_ref, mixd_ref, xp_ref, xd_ref, wo_ref, g1_ref, b1_ref, wg_ref, wu_ref, wd_ref,
                       g2_ref, b2_ref, op_ref, od_ref):
    def tail(mix_ref, x_ref, o_ref):
        x1 = _mixer_residual_norm(mix_ref, x_ref, wo_ref, g1_ref, b1_ref)
        f = _swiglu(x1.astype(BF16), wg_ref[...], wu_ref[...], wd_ref[...])
        o_ref[...] = _layernorm(ALPHA * x1 + f, g2_ref[...], b2_ref[...])

    _on_token_tile(tail, (mixp_ref, xp_ref, op_ref), (mixd_ref, xd_ref, od_ref))


def _dense_tail(mix_p, mix_d, x_p, x_d, wo, layer, g1, b1, wg, wu, wd, g2, b2):
    return pl.pallas_call(
        _dense_tail_kernel,
        grid=(N_TILES,),
        in_specs=[_p_spec(D_MIX), _d_spec(D_MIX), _p_spec(D_MODEL), _d_spec(D_MODEL), _layer_spec(wo, layer),
                  _const_spec(g1), _const_spec(b1), WHOLE_VMEM, WHOLE_VMEM, WHOLE_VMEM,
                  _const_spec(g2), _const_spec(b2)],
        out_specs=[_p_spec(D_MODEL), _d_spec(D_MODEL)],
        out_shape=[jax.ShapeDtypeStruct((N_PROMPT, D_MODEL), F32), jax.ShapeDtypeStruct((DEC_BATCH, D_MODEL), F32)],
        compiler_params=TOKEN_PARAMS, name="dense_tail",
    )(mix_p, mix_d, x_p, x_d, wo, g1, b1, wg, wu, wd, g2, b2)


TOP_K = 2
TMOE = 512
MOE_TILES = (TOP_K * N_TOK) // TMOE + N_EXPERTS
MOE_ROWS = MOE_TILES * TMOE


def _route(x, rhi_ref, rlo_ref, base, route_ref, gate_ref):
    n = x.shape[0]
    lane_i = lax.broadcasted_iota(jnp.int32, (n, LANES), 1)
    lane = lane_i.astype(F32)
    x_hi = x.astype(BF16)
    x_lo = (x - x_hi.astype(F32)).astype(BF16)
    r_hi = rhi_ref[0]
    logits = (jnp.dot(x_hi, r_hi, preferred_element_type=F32)
              + (jnp.dot(x_lo, r_hi, preferred_element_type=F32)
                 + jnp.dot(x_hi, rlo_ref[0], preferred_element_type=F32)))
    logits = jnp.where(lane_i < N_EXPERTS, logits, -jnp.inf)
    v1 = jnp.max(logits, -1, keepdims=True)
    i1 = jnp.min(jnp.where(logits == v1, lane, float(LANES)), -1, keepdims=True)
    rest = jnp.where(lane == i1, -jnp.inf, logits)
    v2 = jnp.max(rest, -1, keepdims=True)
    i2 = jnp.min(jnp.where(rest == v2, lane, float(LANES)), -1, keepdims=True)
    e2 = jnp.exp(v2 - v1)
    g1 = 1.0 / (1.0 + e2)
    oh1 = jnp.where(lane == i1, 1.0, 0.0)
    oh2 = jnp.where(lane == i2, 1.0, 0.0)
    tr = lax.broadcasted_iota(jnp.int32, (n, n), 0)
    tc = lax.broadcasted_iota(jnp.int32, (n, n), 1)
    before = jnp.where(tc < tr, 1.0, 0.0).astype(BF16)
    cum1 = jnp.dot(before, oh1.astype(BF16), preferred_element_type=F32)
    cum2 = jnp.dot(before, oh2.astype(BF16), preferred_element_type=F32)
    tot1 = jnp.sum(oh1, axis=0, keepdims=True)
    tot2 = jnp.sum(oh2, axis=0, keepdims=True)
    b = base[...]
    rank1 = jnp.sum(oh1 * (cum1 + b), -1, keepdims=True)
    rank2 = jnp.sum(oh2 * (cum2 + b + tot1), -1, keepdims=True)
    base[...] = b + tot1 + tot2
    route = jnp.where(lane_i == 0, i1, jnp.where(lane_i == 1, i2, jnp.where(lane_i == 2, rank1,
                                                                            jnp.where(lane_i == 3, rank2, 0.0))))
    route_ref[0:n, :] = route.astype(jnp.int32)
    gate_ref[0:n, :] = jnp.where(lane_i == 0, g1, jnp.where(lane_i == 1, e2 * g1, 0.0))


def _moe_head_kernel(mixp_ref, mixd_ref, xp_ref, xd_ref, wo_ref, g1_ref, b1_ref, rhi_ref, rlo_ref,
                     wg_ref, wu_ref, wd_ref,
                     x1p_ref, x1d_ref, route_ref, gate_ref, cnt_ref, wgb_ref, wub_ref, wdb_ref, base):
    @pl.when(pl.program_id(0) == 0)
    def _():
        base[...] = jnp.zeros_like(base)

    def head(mix_ref, x_ref, x1_ref):
        x1 = _mixer_residual_norm(mix_ref, x_ref, wo_ref, g1_ref, b1_ref)
        x1_ref[...] = x1
        _route(x1, rhi_ref, rlo_ref, base, route_ref, gate_ref)
        if x_ref is xp_ref:
            wgb_ref[...] = wg_ref[0].astype(BF16)
            wub_ref[...] = wu_ref[0].astype(BF16)
            wdb_ref[...] = wd_ref[0].astype(BF16)

    _on_token_tile(head, (mixp_ref, xp_ref, x1p_ref), (mixd_ref, xd_ref, x1d_ref))
    cnt_ref[...] = jnp.broadcast_to(base[...], cnt_ref.shape).astype(jnp.int32)


def _moe_head(mix_p, mix_d, x_p, x_d, wo, layer, g1, b1, r_hi, r_lo, moe_layer, wg, wu, wd):
    specs = [_cast_specs(w, moe_layer, N_PROMPT_TILES) for w in (wg, wu, wd)]
    return pl.pallas_call(
        _moe_head_kernel,
        grid=(N_TILES,),
        in_specs=[_p_spec(D_MIX), _d_spec(D_MIX), _p_spec(D_MODEL), _d_spec(D_MODEL), _layer_spec(wo, layer),
                  _const_spec(g1), _const_spec(b1), _layer_spec(r_hi, moe_layer), _layer_spec(r_lo, moe_layer)]
                 + [s[0] for s in specs],
        out_specs=[_p_spec(D_MODEL), _d_spec(D_MODEL), _tile_spec(LANES), _tile_spec(LANES),
                   pl.BlockSpec((8, LANES), lambda i: (0, 0))] + [s[1] for s in specs],
        out_shape=[jax.ShapeDtypeStruct((N_PROMPT, D_MODEL), F32), jax.ShapeDtypeStruct((DEC_BATCH, D_MODEL), F32),
                   jax.ShapeDtypeStruct((N_TOK, LANES), jnp.int32), jax.ShapeDtypeStruct((N_TOK, LANES), F32),
                   jax.ShapeDtypeStruct((8, LANES), jnp.int32)]
                  + [jax.ShapeDtypeStruct(w.shape[1:], BF16) for w in (wg, wu, wd)],
        scratch_shapes=[pltpu.VMEM((1, LANES), F32)],
        compiler_params=TOKEN_PARAMS, name="moe_head",
    )(mix_p, mix_d, x_p, x_d, wo, g1, b1, r_hi, r_lo, wg, wu, wd)


def _row_copy(src, src_row, dst, dst_row, sem):
    return pltpu.make_async_copy(src.at[pl.ds(src_row, 1)], dst.at[pl.ds(dst_row, 1)], sem)


DMA_UNROLL = 8


def _dispatch_kernel(pos1_ref, pos2_ref, cnt_ref, off_ref, pad_ref, xp_ref, xd_ref, xs_hbm, sem):
    start = pl.program_id(0) * TM

    def scatter(x_ref):
        n = x_ref.shape[0]

        def issue(r, carry):
            t = start + r
            _row_copy(x_ref, r, xs_hbm, pos1_ref[t], sem).start(priority=0)
            _row_copy(x_ref, r, xs_hbm, pos2_ref[t], sem).start(priority=1)
            return carry

        lax.fori_loop(0, n, issue, 0, unroll=DMA_UNROLL)
        for _ in range(TOP_K):
            pltpu.make_async_copy(x_ref, xs_hbm.at[pl.ds(0, n)], sem).wait()

    def scatter_and_pad(x_ref):
        scatter(x_ref)

        def drain(q, carry):
            _row_copy(x_ref, 0, xs_hbm, 0, sem).wait()
            return carry

        for e in range(N_EXPERTS):
            lo = off_ref[e] + cnt_ref[e]

            def fill(q, carry, lo=lo):
                _row_copy(x_ref, 0, xs_hbm, lo + q, sem).start()
                return carry

            lax.fori_loop(0, pad_ref[e], fill, 0)
            lax.fori_loop(0, pad_ref[e], drain, 0)

    _on_token_tile(lambda x_ref, pad: scatter_and_pad(x_ref) if pad else scatter(x_ref),
                   (xp_ref, False), (xd_ref, True))

def _dispatch(x_p, x_d, pos1, pos2, counts, offs, pads):
    return pl.pallas_call(
        _dispatch_kernel,
        grid_spec=pltpu.PrefetchScalarGridSpec(
            num_scalar_prefetch=5, grid=(N_TILES,),
            in_specs=[_p_spec(D_MODEL), _d_spec(D_MODEL)],
            out_specs=pl.BlockSpec(memory_space=pl.ANY),
            scratch_shapes=[pltpu.SemaphoreType.DMA(())]),
        out_shape=jax.ShapeDtypeStruct((MOE_ROWS, D_MODEL), F32),
        compiler_params=TOKEN_PARAMS, name="moe_dispatch",
    )(pos1, pos2, counts, offs, pads, x_p, x_d)


def _gffn_kernel(te_ref, nv_ref, xs_ref, wg_ref, wu_ref, wd_ref, ys_ref):
    @pl.when(pl.program_id(0) < nv_ref[0])
    def _():
        ys_ref[...] = _swiglu(xs_ref[...].astype(BF16), wg_ref[0], wu_ref[0], wd_ref[0])


def _gffn(xs, tile_expert, n_valid, wg, wu, wd):
    def tile(i, te, nv):
        return (jnp.minimum(i, nv[0] - 1), 0)

    def expert(i, te, nv):
        return (te[jnp.minimum(i, nv[0] - 1)], 0, 0)

    return pl.pallas_call(
        _gffn_kernel,
        grid_spec=pltpu.PrefetchScalarGridSpec(
            num_scalar_prefetch=2, grid=(MOE_TILES,),
            in_specs=[pl.BlockSpec((TMOE, D_MODEL), tile), pl.BlockSpec((1, D_MODEL, D_FF), expert),
                      pl.BlockSpec((1, D_MODEL, D_FF), expert), pl.BlockSpec((1, D_FF, D_MODEL), expert)],
            out_specs=pl.BlockSpec((TMOE, D_MODEL), tile)),
        out_shape=jax.ShapeDtypeStruct((MOE_ROWS, D_MODEL), F32),
        compiler_params=pltpu.CompilerParams(dimension_semantics=("arbitrary",), vmem_limit_bytes=VMEM_LIMIT),
        name="moe_ffn",
    )(tile_expert, n_valid, xs, wg, wu, wd)


def _combine_kernel(pos1_ref, pos2_ref, xp_ref, xd_ref, gate_ref, ys_hbm, g_ref, b_ref, op_ref, od_ref, rows, sem):
    start = pl.program_id(0) * TM

    def combine(x_ref, o_ref):
        n = x_ref.shape[0]

        def issue(r, carry):
            t = start + r
            pltpu.make_async_copy(ys_hbm.at[pl.ds(pos1_ref[t], 1)], rows.at[0, pl.ds(r, 1)], sem).start(priority=0)
            pltpu.make_async_copy(ys_hbm.at[pl.ds(pos2_ref[t], 1)], rows.at[1, pl.ds(r, 1)], sem).start(priority=1)
            return carry

        lax.fori_loop(0, n, issue, 0, unroll=DMA_UNROLL)
        for s in range(TOP_K):
            pltpu.make_async_copy(ys_hbm.at[pl.ds(0, n)], rows.at[s, pl.ds(0, n)], sem).wait()
        gate = gate_ref[0:n, :]
        f = gate[:, 0:1] * rows[0, 0:n, :] + gate[:, 1:2] * rows[1, 0:n, :]
        o_ref[...] = _layernorm(ALPHA * x_ref[...] + f, g_ref[...], b_ref[...])

    _on_token_tile(combine, (xp_ref, op_ref), (xd_ref, od_ref))


def _combine_ln(x_p, x_d, gates, ys, pos1, pos2, g, b):
    return pl.pallas_call(
        _combine_kernel,
        grid_spec=pltpu.PrefetchScalarGridSpec(
            num_scalar_prefetch=2, grid=(N_TILES,),
            in_specs=[_p_spec(D_MODEL), _d_spec(D_MODEL), _tile_spec(LANES), pl.BlockSpec(memory_space=pl.ANY),
                      _const_spec(g), _const_spec(b)],
            out_specs=[_p_spec(D_MODEL), _d_spec(D_MODEL)],
            scratch_shapes=[pltpu.VMEM((TOP_K, TM, D_MODEL), F32), pltpu.SemaphoreType.DMA(())]),
        out_shape=[jax.ShapeDtypeStruct((N_PROMPT, D_MODEL), F32), jax.ShapeDtypeStruct((DEC_BATCH, D_MODEL), F32)],
        compiler_params=TOKEN_PARAMS, name="moe_combine_ln",
    )(pos1, pos2, x_p, x_d, gates, ys, g, b)


def _moe_tail(mix_p, mix_d, x_p, x_d, wo, layer, g1, b1, r_hi, r_lo, moe_layer, wg, wu, wd, g2, b2):
    flat = [w.reshape(w.shape[0], -1, w.shape[-1]) for w in (wg, wu, wd)]
    x1_p, x1_d, route, gates, cnt, wg_b, wu_b, wd_b = _moe_head(mix_p, mix_d, x_p, x_d, wo, layer, g1, b1,
                                                                r_hi, r_lo, moe_layer, *flat)
    wg_b, wu_b, wd_b = (o.reshape(w.shape[1:]) for o, w in zip((wg_b, wu_b, wd_b), (wg, wu, wd)))
    counts = cnt[0, :N_EXPERTS]
    padded = (counts + TMOE - 1) // TMOE * TMOE
    ends = jnp.cumsum(padded)
    offs = ends - padded
    pos1 = jnp.take(offs, route[:, 0]) + route[:, 2]
    pos2 = jnp.take(offs, route[:, 1]) + route[:, 3]
    tile_ends = ends // TMOE
    tile_ids = jnp.arange(MOE_TILES, dtype=jnp.int32)
    tile_expert = jnp.minimum(jnp.sum(tile_ids[:, None] >= tile_ends[None, :], axis=1), N_EXPERTS - 1)
    xs = _dispatch(x1_p, x1_d, pos1, pos2, counts, offs, padded - counts)
    ys = _gffn(xs, tile_expert.astype(jnp.int32), tile_ends[N_EXPERTS - 1:], wg_b, wu_b, wd_b)
    return _combine_ln(x1_p, x1_d, gates, ys, pos1, pos2, g2, b2)


def _rope_tables(pos):
    half = HEAD_DIM // 2
    inv_freq = ROPE_THETA ** (-jnp.arange(half, dtype=F32) / half)
    ang = pos.astype(F32)[:, None] * inv_freq[None, :]
    cos = jnp.cos(ang)
    sin = jnp.sin(ang)
    return jnp.concatenate([cos] * 4, axis=1), jnp.concatenate([-sin, sin, -sin, sin], axis=1)


def _pad_lanes(a):
    return jnp.pad(a, [(0, 0)] * (a.ndim - 1) + [(0, LANES - a.shape[-1])])


def kernel(x_prompt, x_sample, cache_win_k, cache_win_v, state_ssd, state_ssd_conv, state_conv, w_in, attn_sinks,
           ssd_conv_w, ssd_conv_b, ssd_dt_bias, ssd_a_log, ssd_d, ssd_norm_w, sconv_w, w_out, ln1_g, ln1_b, ln2_g,
           ln2_b, ffn_w_gate, ffn_w_up, ffn_w_down, moe_router, moe_w_gate, moe_w_up, moe_w_down):
    x_p = x_prompt.reshape(N_PROMPT, D_MODEL)
    x_d = x_sample.reshape(DEC_BATCH, D_MODEL)
    cos_p, sin_p = _rope_tables(jnp.arange(SEQ, dtype=jnp.int32))
    cos_s, sin_s = _rope_tables(PAST_LEN + jnp.arange(1, dtype=jnp.int32))
    split = C_GB + SSD_HEADS
    w_in_r = jnp.concatenate([w_in[:, :, :C_GB], w_in[:, :, split:], _pad_lanes(w_in[:, :, C_GB:split])],
                             axis=-1).astype(BF16)
    w_out_b = w_out.astype(BF16)
    dtb = _pad_lanes(ssd_dt_bias)[:, None, :]
    alog = _pad_lanes(ssd_a_log)[:, None, :]
    dskip = jnp.repeat(ssd_d, SSD_HEAD_DIM, axis=-1)[:, None, :]
    router = _pad_lanes(moe_router)
    r_hi = router.astype(BF16)
    r_lo = (router - r_hi.astype(F32)).astype(BF16)
    sink_rows = jnp.broadcast_to(attn_sinks[:, :, None], (DEPTH, N_HEADS, LANES))
    kc = cache_win_k.reshape(DEPTH, DEC_BATCH, WINDOW, D_KV)
    vc = cache_win_v.reshape(DEPTH, DEC_BATCH, WINDOW, D_KV)
    st = state_ssd.reshape(DEPTH, DEC_BATCH, SSD_HEADS * D_STATE_FLAT)
    sbuf = state_ssd_conv.reshape(DEPTH, DEC_BATCH, (SSD_CONV - 1) * D_XBC)
    cbuf = state_conv.reshape(DEPTH, DEC_BATCH, (CONV_WIDTH - 1) * D_CONV)

    outs = {n: [] for n in ("pk", "pv", "ph", "psc", "pc", "sk", "sv", "sh", "ssc", "sc")}
    for i in range(DEPTH):
        j = i // 2
        dense = i % 2 == 0
        if dense:
            proj_p, proj_d, wg_b, wu_b, wd_b = _inproj(x_p, x_d, w_in_r, i, (ffn_w_gate, ffn_w_up, ffn_w_down), j)
        else:
            proj_p, proj_d = _inproj(x_p, x_d, w_in_r, i)
        proj_p = proj_p.reshape(BATCH, SEQ, D_PROJ)
        params = (ssd_conv_w[i], ssd_conv_b[i][None, :], dtb[i], alog[i], dskip[i], ssd_norm_w[i][None, :],
                  sconv_w[i])
        mix_p, klast, hlast, culast = _mixer_prompt(proj_p, attn_sinks[i], cos_p, sin_p, *params)
        oattn, knew = _dec_attn(proj_d, kc, vc, i, cos_s, sin_s, sink_rows[i])
        mix_d, stnew, sbufnew, cbufnew = _dec_ssd(oattn, proj_d, sbuf, cbuf, st, i, *params)
        mix_p = mix_p.reshape(N_PROMPT, D_MIX)
        ln1 = (ln1_g[i][None, :], ln1_b[i][None, :])
        ln2 = (ln2_g[i][None, :], ln2_b[i][None, :])
        if dense:
            x_p, x_d = _dense_tail(mix_p, mix_d, x_p, x_d, w_out_b, i, *ln1, wg_b, wu_b, wd_b, *ln2)
        else:
            x_p, x_d = _moe_tail(mix_p, mix_d, x_p, x_d, w_out_b, i, *ln1, r_hi, r_lo, j,
                                 moe_w_gate, moe_w_up, moe_w_down, *ln2)

        def tail(n_rows, col, width):
            return proj_p[:, SEQ - n_rows:, col:col + width]

        outs["pk"].append(klast.reshape(BATCH, WINDOW, N_KV_HEADS, HEAD_DIM))
        outs["pv"].append(tail(WINDOW, C_V, D_KV).reshape(BATCH, WINDOW, N_KV_HEADS, HEAD_DIM))
        outs["ph"].append(hlast)
        outs["psc"].append(tail(SSD_CONV - 1, C_XBC, D_XBC))
        outs["pc"].append(culast[:, 8 - (CONV_WIDTH - 1):, :])
        outs["sk"].append(knew)
        outs["sv"].append(proj_d[:, C_V:C_V + D_KV])
        outs["sh"].append(stnew.reshape(DEC_BATCH, SSD_HEADS, SSD_HEAD_DIM, SSD_STATE))
        outs["ssc"].append(sbufnew.reshape(DEC_BATCH, SSD_CONV - 1, D_XBC))
        outs["sc"].append(cbufnew.reshape(DEC_BATCH, CONV_WIDTH - 1, D_CONV))
    y_prompt = x_p.reshape(BATCH, SEQ, D_MODEL)
    y_sample = x_d.reshape(DEC_BATCH, 1, D_MODEL)
    stk = lambda n: jnp.stack(outs[n])

    def slide(cache, new):
        win = jnp.concatenate([cache[:, :, 1:], stk(new)[:, :, None, :]], axis=2)
        return win.reshape(DEPTH, DEC_BATCH, WINDOW, N_KV_HEADS, HEAD_DIM)

    return (y_prompt, y_sample, stk("pk"), stk("pv"), stk("ph"), stk("psc"), stk("pc"),
            slide(kc, "sk"), slide(vc, "sv"), stk("sh"), stk("ssc"), stk("sc"))
```

```python
import functools
import math

import jax
import jax.numpy as jnp
from jax import lax
from jax.experimental import pallas as pl
from jax.experimental.pallas import tpu as pltpu

D_MODEL = 1024
BATCH = 2
SEQ = 8192
DEPTH = 4
DEC_BATCH = 128
PAST_LEN = 8192
N_HEADS = 8
N_KV_HEADS = 2
HEAD_DIM = 64
Q_PER_KV = N_HEADS // N_KV_HEADS
D_ATTN = N_HEADS * HEAD_DIM
D_KV = N_KV_HEADS * HEAD_DIM
WINDOW = 128
BLK = 128
ROPE_THETA = 10000.0
SSD_HEADS = 4
SSD_HEAD_DIM = 64
D_SSD = SSD_HEADS * SSD_HEAD_DIM
SSD_GROUPS = 2
SSD_STATE = 64
SSD_CONV = 4
D_XBC = D_SSD + 2 * SSD_GROUPS * SSD_STATE
D_CONV = 256
CONV_WIDTH = 3
D_MIX = D_ATTN + D_SSD + D_CONV
D_FF = 2816
N_EXPERTS = 8
ALPHA = (2 * DEPTH) ** 0.25
NORM_EPS = 1e-5

LANES = 128
N_PROMPT = BATCH * SEQ
N_TOK = N_PROMPT + DEC_BATCH
N_BLK = SEQ // BLK
TM = 512
N_TILES = pl.cdiv(N_TOK, TM)
VMEM_LIMIT = 56 * 1024 * 1024

C_Q, C_K, C_V, C_Z, C_XBC = 0, 512, 640, 768, 1024
C_GB, C_GC, C_U, C_DT = 1536, 1792, 2048, 2304
D_PROJ = C_DT + LANES

BF16 = jnp.bfloat16
F32 = jnp.float32


def _sigmoid(x):
    return 1.0 / (1.0 + jnp.exp(-x))


def _silu(x):
    return x * _sigmoid(x)


def _softplus(x):
    return jnp.maximum(x, 0.0) + jnp.log1p(jnp.exp(-jnp.abs(x)))


def _layernorm(xf, g, b):
    mu = jnp.mean(xf, -1, keepdims=True)
    xc = xf - mu
    var = jnp.mean(xc * xc, -1, keepdims=True)
    return xc * lax.rsqrt(var + NORM_EPS) * g + b


def _rope(x, cos, sin):
    w = x.shape[1]
    reps = w // LANES
    if reps > 1:
        cos = jnp.concatenate([cos] * reps, axis=1)
        sin = jnp.concatenate([sin] * reps, axis=1)
    lane = lax.broadcasted_iota(jnp.int32, x.shape, 1)
    first_half = (lane % HEAD_DIM) < (HEAD_DIM // 2)
    partner = jnp.where(first_half, pltpu.roll(x, w - HEAD_DIM // 2, 1), pltpu.roll(x, HEAD_DIM // 2, 1))
    return x * cos + partner * sin


def _dot(a, b):
    return jnp.dot(a.astype(BF16), b.astype(BF16), preferred_element_type=F32)


def _dot_nt(a, b):
    return lax.dot_general(a.astype(BF16), b.astype(BF16), (((1,), (1,)), ((), ())), preferred_element_type=F32)


def _dot_tn(a, b):
    return lax.dot_general(a.astype(BF16), b.astype(BF16), (((0,), (0,)), ((), ())), preferred_element_type=F32)


N_PROMPT_TILES = N_PROMPT // TM
TOKEN_PARAMS = pltpu.CompilerParams(dimension_semantics=("arbitrary",), vmem_limit_bytes=VMEM_LIMIT)
WHOLE_VMEM = pl.BlockSpec(memory_space=pltpu.VMEM)


def _p_spec(width):
    return pl.BlockSpec((TM, width), lambda i, *_: (jnp.minimum(i, N_PROMPT_TILES - 1), 0))


def _d_spec(width):
    return pl.BlockSpec((DEC_BATCH, width), lambda i, *_: (0, 0))


def _tile_spec(width):
    return pl.BlockSpec((TM, width), lambda i, *_: (i, 0))


def _const_spec(a):
    return pl.BlockSpec(a.shape, lambda i, *_: (0,) * a.ndim)


def _layer_spec(a, layer):
    return pl.BlockSpec((1,) + a.shape[1:], lambda i, *_: (layer,) + (0,) * (a.ndim - 1))


def _on_token_tile(fn, prompt_refs, decode_refs):
    i = pl.program_id(0)

    @pl.when(i < N_PROMPT_TILES)
    def _():
        fn(*prompt_refs)

    @pl.when(i == N_PROMPT_TILES)
    def _():
        fn(*decode_refs)


def _cast_specs(w, layer, steps):
    _, rows, cols = w.shape
    return (pl.BlockSpec((1, rows // steps, cols), lambda i, *_: (layer, jnp.minimum(i, steps - 1), 0)),
            pl.BlockSpec((rows // steps, cols), lambda i, *_: (jnp.minimum(i, steps - 1), 0)))


FFN_CAST_STEPS = 16


def _inproj_kernel(xp_ref, xd_ref, w_ref, *refs):
    if len(refs) == 2:
        pp_ref, pd_ref = refs
    else:
        wg_ref, wu_ref, wd_ref, pp_ref, pd_ref, wgb_ref, wub_ref, wdb_ref = refs

    def project(x_ref, o_ref):
        o_ref[...] = jnp.dot(x_ref[...].astype(BF16), w_ref[0], preferred_element_type=F32)

    _on_token_tile(project, (xp_ref, pp_ref), (xd_ref, pd_ref))
    if len(refs) > 2:
        @pl.when(pl.program_id(0) < FFN_CAST_STEPS)
        def _():
            wgb_ref[...] = wg_ref[0].astype(BF16)
            wub_ref[...] = wu_ref[0].astype(BF16)
            wdb_ref[...] = wd_ref[0].astype(BF16)


def _inproj(x_p, x_d, w, layer, ffn_w=None, ffn_layer=0):
    in_specs = [_p_spec(D_MODEL), _d_spec(D_MODEL), _layer_spec(w, layer)]
    out_specs = [_p_spec(D_PROJ), _d_spec(D_PROJ)]
    out_shape = [jax.ShapeDtypeStruct((N_PROMPT, D_PROJ), F32), jax.ShapeDtypeStruct((DEC_BATCH, D_PROJ), F32)]
    args = [x_p, x_d, w]
    if ffn_w is not None:
        specs = [_cast_specs(a, ffn_layer, FFN_CAST_STEPS) for a in ffn_w]
        in_specs += [s[0] for s in specs]
        out_specs += [s[1] for s in specs]
        out_shape += [jax.ShapeDtypeStruct(a.shape[1:], BF16) for a in ffn_w]
        args += list(ffn_w)
    return pl.pallas_call(
        _inproj_kernel, grid=(N_TILES,), in_specs=in_specs, out_specs=out_specs, out_shape=out_shape,
        compiler_params=TOKEN_PARAMS, name="inproj",
    )(*args)


def _ssd_gate_norm(y, z, norm_w):
    y = y * _silu(z)
    return y * lax.rsqrt(jnp.mean(y * y, -1, keepdims=True) + NORM_EPS) * norm_w


def _attention_block(b, sinks_ref, q, k, v, bias, kbd, vbd, mix_ref):
    lo = lax.broadcasted_iota(jnp.int32, (BLK, LANES), 1) < HEAD_DIM
    k_sw = pltpu.roll(k, HEAD_DIM, 1)
    v_sw = pltpu.roll(v, HEAD_DIM, 1)
    for g in range(N_KV_HEADS):
        for ref, x, x_sw in ((kbd, k, k_sw), (vbd, v, v_sw)):
            a_src, b_src = (x, x_sw) if g == 0 else (x_sw, x)
            ref[b, g, 0:BLK, :] = ref[b, g, BLK:2 * BLK, :]
            ref[b, g, 2 * BLK:3 * BLK, :] = ref[b, g, 3 * BLK:4 * BLK, :]
            ref[b, g, BLK:2 * BLK, :] = jnp.where(lo, a_src, 0.0).astype(BF16)
            ref[b, g, 3 * BLK:4 * BLK, :] = jnp.where(lo, 0.0, b_src).astype(BF16)
    for m in range(N_HEADS // 2):
        g = (2 * m) // Q_PER_KV
        s = _dot_nt(q[:, m * LANES:(m + 1) * LANES], kbd[b, g]) + bias
        probs, inv = [], []
        for hh in range(2):
            sh = s[:, hh * 2 * BLK:(hh + 1) * 2 * BLK]
            sink = sinks_ref[2 * m + hh]
            mx = jnp.maximum(jnp.max(sh, -1, keepdims=True), sink)
            p = jnp.exp(sh - mx)
            inv.append(1.0 / (jnp.sum(p, -1, keepdims=True) + jnp.exp(sink - mx)))
            probs.append(p.astype(BF16))
        o = jnp.dot(jnp.concatenate(probs, axis=1), vbd[b, g], preferred_element_type=F32)
        mix_ref[b, :, m * LANES:(m + 1) * LANES] = (o * jnp.where(lo, inv[0], inv[1])).astype(mix_ref.dtype)


def _mixer_prompt_kernel(sinks_ref, proj_ref,
                         cos_ref, sin_ref, bias_ref, cw_ref, cb_ref, dtb_ref, alog_ref, dskip_ref, nw_ref, scw_ref,
                         mix_ref, klast_ref, hlast_ref, culast_ref,
                         kbd, vbd, conv_ext, cu_ext, hstate):
    j = pl.program_id(0)
    cw_ref, cb_ref, dtb_ref, alog_ref, dskip_ref, nw_ref, scw_ref = (
        r.at[0] for r in (cw_ref, cb_ref, dtb_ref, alog_ref, dskip_ref, nw_ref, scw_ref))

    @pl.when(j == 0)
    def _():
        kbd[...] = jnp.zeros_like(kbd)
        vbd[...] = jnp.zeros_like(vbd)
        conv_ext[:, 0:8, :] = jnp.zeros((BATCH, 8, D_XBC), F32)
        cu_ext[:, 0:8, :] = jnp.zeros((BATCH, 8, D_CONV), F32)
        hstate[...] = jnp.zeros_like(hstate)

    for sub in range(MIX_SUB):
        rows = pl.ds(sub * BLK, BLK)

        def cols(start, width, rows=rows):
            return proj_ref.at[:, rows, pl.ds(start, width)]

        q_ref, k_ref, v_ref, z_ref = cols(C_Q, D_ATTN), cols(C_K, D_KV), cols(C_V, D_KV), cols(C_Z, D_SSD)
        xbc_ref, dt_ref = cols(C_XBC, D_XBC), cols(C_DT, LANES)
        gb_ref, gc_ref, u_ref = cols(C_GB, D_CONV), cols(C_GC, D_CONV), cols(C_U, D_CONV)
        mix_blk = mix_ref.at[:, rows]
        cos = cos_ref[rows, :]
        sin = sin_ref[rows, :]
        bias = bias_ref[jnp.minimum(j * MIX_SUB + sub, 1)]
        for b in range(BATCH):
            q = _rope(q_ref[b], cos, sin) * (1.0 / math.sqrt(HEAD_DIM))
            k = _rope(k_ref[b], cos, sin)
            if sub == MIX_SUB - 1:
                klast_ref[b] = k
            _attention_block(b, sinks_ref, q, k, v_ref[b], bias, kbd, vbd, mix_blk)
            _ssd_block(b, z_ref, xbc_ref, dt_ref, cw_ref, cb_ref, dtb_ref, alog_ref, dskip_ref, nw_ref,
                       mix_blk, hlast_ref, conv_ext, hstate)
            _conv_block(b, gb_ref, gc_ref, u_ref, scw_ref, mix_blk, culast_ref, cu_ext)


def _ssd_block(b, z_ref, xbc_ref, dt_ref, cw_ref, cb_ref, dtb_ref, alog_ref, dskip_ref, nw_ref,
               mix_ref, hlast_ref, conv_ext, hstate):
    conv_ext[b, 8:8 + BLK, :] = xbc_ref[b]
    acc = conv_ext[b, 5:5 + BLK, :] * cw_ref[0:1, :]
    for t in range(1, SSD_CONV):
        acc = acc + conv_ext[b, 5 + t:5 + t + BLK, :] * cw_ref[t:t + 1, :]
    conv_ext[b, 0:8, :] = conv_ext[b, BLK:BLK + 8, :]
    xbc = _silu(acc + cb_ref[...])
    xs = xbc[:, 0:D_SSD]
    bm = xbc[:, D_SSD:D_SSD + LANES]
    cm = xbc[:, D_SSD + LANES:D_SSD + 2 * LANES]
    dt = _softplus(dt_ref[b] + dtb_ref[...])
    dta = dt * (-jnp.exp(alog_ref[...]))
    ti = lax.broadcasted_iota(jnp.int32, (BLK, BLK), 0)
    si = lax.broadcasted_iota(jnp.int32, (BLK, BLK), 1)
    causal = si <= ti
    cum = jnp.dot(causal.astype(F32), dta, preferred_element_type=F32, precision=lax.Precision.HIGHEST)
    cum_t = cum.T
    ys = []
    for g in range(SSD_GROUPS):
        sl = slice(g * SSD_STATE, (g + 1) * SSD_STATE)
        b_g = bm[:, sl]
        c_g = cm[:, sl]
        cb_scores = _dot_nt(c_g, b_g)
        for hh in range(SSD_HEADS // SSD_GROUPS):
            h = g * (SSD_HEADS // SSD_GROUPS) + hh
            ccol = cum[:, h:h + 1]
            crow = cum_t[h:h + 1, :]
            clast = cum[BLK - 1:BLK, h:h + 1]
            decay = jnp.exp(jnp.where(causal, ccol - crow, -jnp.inf))
            x_h = xs[:, h * SSD_HEAD_DIM:(h + 1) * SSD_HEAD_DIM]
            xdt = x_h * dt[:, h:h + 1]
            h_prev = hstate[b, h]
            y = _dot(cb_scores * decay, xdt)
            y = y + _dot_nt(c_g, h_prev) * jnp.exp(ccol)
            states = _dot_tn(xdt * jnp.exp(clast - ccol), b_g)
            hstate[b, h] = h_prev * jnp.exp(clast) + states
            ys.append(y)
    y = jnp.concatenate(ys, axis=1) + dskip_ref[...] * xs
    mix_ref[b, :, D_ATTN:D_ATTN + D_SSD] = _ssd_gate_norm(y, z_ref[b], nw_ref[...]).astype(mix_ref.dtype)
    hlast_ref[b] = hstate[b]


def _conv_block(b, gb_ref, gc_ref, u_ref, scw_ref, mix_ref, culast_ref, cu_ext):
    cu_ext[b, 8:8 + BLK, :] = gc_ref[b] * u_ref[b]
    cc = cu_ext[b, 6:6 + BLK, :] * scw_ref[0:1, :]
    for t in range(1, CONV_WIDTH):
        cc = cc + cu_ext[b, 6 + t:6 + t + BLK, :] * scw_ref[t:t + 1, :]
    tail = cu_ext[b, BLK:BLK + 8, :]
    cu_ext[b, 0:8, :] = tail
    culast_ref[b] = tail
    mix_ref[b, :, D_ATTN + D_SSD:D_MIX] = (gb_ref[b] * cc).astype(mix_ref.dtype)


MIX_SUB = 2


def _attn_bias():
    r = jnp.arange(BLK, dtype=jnp.int32)[:, None]
    c = jnp.arange(4 * BLK, dtype=jnp.int32)[None, :] % (2 * BLK)
    window = (c >= r) & (c - BLK <= r)
    first = window & (c >= BLK)
    return jnp.where(jnp.stack([first, window]), 0.0, -jnp.inf).astype(F32)


def _mixer_prompt(proj, sinks, cos_tab, sin_tab, layer, cw, cb, dtb, alog, dskip, nw, scw):
    def full(a):
        return pl.BlockSpec(a.shape, lambda j, *_: (0,) * a.ndim)

    step = MIX_SUB * BLK
    bias = _attn_bias()
    params = (cw, cb, dtb, alog, dskip, nw, scw)
    out_shape = [jax.ShapeDtypeStruct((BATCH, SEQ, D_MIX), BF16),
                 jax.ShapeDtypeStruct((BATCH, BLK, D_KV), F32),
                 jax.ShapeDtypeStruct((BATCH, SSD_HEADS, SSD_HEAD_DIM, SSD_STATE), F32),
                 jax.ShapeDtypeStruct((BATCH, 8, D_CONV), F32)]
    grid_spec = pltpu.PrefetchScalarGridSpec(
        num_scalar_prefetch=1,
        grid=(SEQ // step,),
        in_specs=[pl.BlockSpec((BATCH, step, D_PROJ), lambda j, *_: (0, j, 0)),
                  pl.BlockSpec((step, LANES), lambda j, *_: (j, 0)),
                  pl.BlockSpec((step, LANES), lambda j, *_: (j, 0)), full(bias)]
                 + [_layer_spec(a, layer) for a in params],
        out_specs=[pl.BlockSpec((BATCH, step, D_MIX), lambda j, *_: (0, j, 0))] + [full(o) for o in out_shape[1:]],
        scratch_shapes=[pltpu.VMEM((BATCH, N_KV_HEADS, 4 * BLK, LANES), BF16),
                        pltpu.VMEM((BATCH, N_KV_HEADS, 4 * BLK, LANES), BF16),
                        pltpu.VMEM((BATCH, BLK + 8, D_XBC), F32), pltpu.VMEM((BATCH, BLK + 8, D_CONV), F32),
                        pltpu.VMEM((BATCH, SSD_HEADS, SSD_HEAD_DIM, SSD_STATE), F32)],
    )
    return pl.pallas_call(
        _mixer_prompt_kernel,
        grid_spec=grid_spec,
        out_shape=out_shape,
        compiler_params=pltpu.CompilerParams(dimension_semantics=("arbitrary",), vmem_limit_bytes=VMEM_LIMIT),
        name="mixer_prompt",
    )(sinks, proj, cos_tab, sin_tab, bias, *params)


DEC_BB = 32


def _dec_attn_kernel(q_ref, k_ref, v_ref, kc_ref, vc_ref, cos_ref, sin_ref, sink_ref,
                     o_ref, knew_ref, qbd, kn8, vn8, o_scr):
    cos = cos_ref[...]
    sin = sin_ref[...]
    q = _rope(q_ref[...], cos, sin) * (1.0 / math.sqrt(HEAD_DIM))
    kn = _rope(k_ref[...], cos, sin)
    vn = v_ref[...]
    knew_ref[...] = kn
    lane = lax.broadcasted_iota(jnp.int32, (DEC_BB, LANES), 1)
    for h in range(N_HEADS):
        g = h // Q_PER_KV
        pair = q[:, (h // 2) * LANES:(h // 2 + 1) * LANES]
        if h % 2 != g:
            pair = pltpu.roll(pair, HEAD_DIM, 1)
        qbd[pl.ds(h, DEC_BB, stride=8), :] = jnp.where(lane // HEAD_DIM == g, pair, 0.0)
        kn8[pl.ds(h, DEC_BB, stride=8), :] = kn
        vn8[pl.ds(h, DEC_BB, stride=8), :] = vn
    q3 = qbd[...].reshape(DEC_BB, N_HEADS, LANES)
    kn3 = kn8[...].reshape(DEC_BB, N_HEADS, LANES)
    vn3 = vn8[...].reshape(DEC_BB, N_HEADS, LANES)
    s3 = jnp.einsum('bhl,bsl->bhs', q3.astype(BF16), kc_ref[0].astype(BF16), preferred_element_type=F32)
    s_self = jnp.sum(q3 * kn3, axis=-1, keepdims=True)
    sink3 = sink_ref[:, :, 0:1]
    m = jnp.maximum(jnp.maximum(jnp.max(s3, -1, keepdims=True), s_self), sink3)
    p3 = jnp.exp(s3 - m)
    p_self = jnp.exp(s_self - m)
    denom = jnp.sum(p3, -1, keepdims=True) + p_self + jnp.exp(sink3 - m)
    o3 = jnp.einsum('bhs,bsl->bhl', p3.astype(BF16), vc_ref[0].astype(BF16), preferred_element_type=F32)
    o3 = (o3 + p_self * vn3) * (1.0 / denom)
    h3 = lax.broadcasted_iota(jnp.int32, o3.shape, 1)
    l3 = lax.broadcasted_iota(jnp.int32, o3.shape, 2)
    o3 = jnp.where(l3 // HEAD_DIM == h3 // Q_PER_KV, o3, 0.0)
    o_scr[...] = o3.reshape(DEC_BB * N_HEADS, LANES)
    for pair in range(N_HEADS // 2):
        g = (2 * pair) // Q_PER_KV
        a = o_scr[pl.ds(2 * pair, DEC_BB, stride=8), :]
        b = o_scr[pl.ds(2 * pair + 1, DEC_BB, stride=8), :]
        if g == 1:
            a = pltpu.roll(a, HEAD_DIM, 1)
        else:
            b = pltpu.roll(b, HEAD_DIM, 1)
        o_ref[:, pair * LANES:(pair + 1) * LANES] = a + b


def _dec_attn(proj, kc, vc, layer, cos1, sin1, sink_rows):
    def pblk(width, col):
        return pl.BlockSpec((DEC_BB, width), lambda i: (i, col // width))

    def full(a):
        return pl.BlockSpec(a.shape, lambda i: (0,) * a.ndim)

    return pl.pallas_call(
        _dec_attn_kernel,
        grid=(DEC_BATCH // DEC_BB,),
        in_specs=[pblk(D_ATTN, C_Q), pblk(D_KV, C_K), pblk(D_KV, C_V),
                  pl.BlockSpec((1, DEC_BB, WINDOW, D_KV), lambda i: (layer, i, 0, 0)),
                  pl.BlockSpec((1, DEC_BB, WINDOW, D_KV), lambda i: (layer, i, 0, 0)),
                  full(cos1), full(sin1), _layer_spec(sink_rows, layer)],
        out_specs=[pl.BlockSpec((DEC_BB, D_ATTN), lambda i: (i, 0)),
                   pl.BlockSpec((DEC_BB, D_KV), lambda i: (i, 0))],
        out_shape=[jax.ShapeDtypeStruct((DEC_BATCH, D_ATTN), F32),
                   jax.ShapeDtypeStruct((DEC_BATCH, D_KV), F32)],
        scratch_shapes=[pltpu.VMEM((DEC_BB * N_HEADS, LANES), F32)] * 4,
        compiler_params=pltpu.CompilerParams(dimension_semantics=("arbitrary",), vmem_limit_bytes=VMEM_LIMIT),
        name="dec_attn",
    )(proj, proj, proj, kc, vc, cos1, sin1, sink_rows)


D_STATE_FLAT = SSD_HEAD_DIM * SSD_STATE


def _dec_ssd_kernel(oattn_ref, z_ref, xbc_ref, gb_ref, gc_ref, u_ref, dt_ref, sbuf_ref, cbuf_ref, st_ref,
                    cw_ref, cb_ref, dtb_ref, alog_ref, dskip_ref, nw_ref, scw_ref,
                    mix_ref, stnew_ref, sbufnew_ref, cbufnew_ref,
                    st_t, stnew_t, xdt_t, b_t, c_t, dec_t, y_t):
    cw_ref, cb_ref, dtb_ref, alog_ref, dskip_ref, nw_ref, scw_ref = (
        r.at[0] for r in (cw_ref, cb_ref, dtb_ref, alog_ref, dskip_ref, nw_ref, scw_ref))
    xbc_new = xbc_ref[...]
    acc = sbuf_ref[0, :,0:D_XBC] * cw_ref[0:1, :]
    for t in range(1, SSD_CONV - 1):
        acc = acc + sbuf_ref[0, :,t * D_XBC:(t + 1) * D_XBC] * cw_ref[t:t + 1, :]
    acc = acc + xbc_new * cw_ref[SSD_CONV - 1:SSD_CONV, :]
    sbufnew_ref[:, 0:(SSD_CONV - 2) * D_XBC] = sbuf_ref[0, :,D_XBC:(SSD_CONV - 1) * D_XBC]
    sbufnew_ref[:, (SSD_CONV - 2) * D_XBC:(SSD_CONV - 1) * D_XBC] = xbc_new
    xbc = _silu(acc + cb_ref[...])
    xs = xbc[:, 0:D_SSD]
    dt = _softplus(dt_ref[...] + dtb_ref[...])
    dta = dt * (-jnp.exp(alog_ref[...]))
    dec_t[...] = jnp.exp(dta).T
    dt_t = dt.T
    xs_t = xs.T
    for h in range(SSD_HEADS):
        sl = slice(h * SSD_HEAD_DIM, (h + 1) * SSD_HEAD_DIM)
        xdt_t[sl, :] = xs_t[sl, :] * dt_t[h:h + 1, :]
    b_t[...] = xbc[:, D_SSD:D_SSD + LANES].T
    c_t[...] = xbc[:, D_SSD + LANES:D_SSD + 2 * LANES].T
    for h in range(SSD_HEADS):
        g = h // (SSD_HEADS // SSD_GROUPS)
        st_t[...] = st_ref[0, :, h * D_STATE_FLAT:(h + 1) * D_STATE_FLAT].T
        dec = dec_t[h:h + 1, :]
        bg = b_t[g * SSD_STATE:(g + 1) * SSD_STATE, :]
        cg = c_t[g * SSD_STATE:(g + 1) * SSD_STATE, :]

        def body(p, carry, h=h, dec=dec, bg=bg, cg=cg):
            off = pl.multiple_of(p * SSD_STATE, SSD_STATE)
            new = st_t[pl.ds(off, SSD_STATE), :] * dec + xdt_t[pl.ds(h * SSD_HEAD_DIM + p, 1), :] * bg
            stnew_t[pl.ds(off, SSD_STATE), :] = new
            y_t[pl.ds(h * SSD_HEAD_DIM + p, 1), :] = jnp.sum(new * cg, axis=0, keepdims=True)
            return carry

        lax.fori_loop(0, SSD_HEAD_DIM, body, 0)
        stnew_ref[:, h * D_STATE_FLAT:(h + 1) * D_STATE_FLAT] = stnew_t[...].T
    y = y_t[...].T + dskip_ref[...] * xs
    mix_ref[:, 0:D_ATTN] = oattn_ref[...].astype(mix_ref.dtype)
    mix_ref[:, D_ATTN:D_ATTN + D_SSD] = _ssd_gate_norm(y, z_ref[...], nw_ref[...]).astype(mix_ref.dtype)

    cu = gc_ref[...] * u_ref[...]
    cc = cbuf_ref[0, :,0:D_CONV] * scw_ref[0:1, :] + cbuf_ref[0, :,D_CONV:2 * D_CONV] * scw_ref[1:2, :]
    cc = cc + cu * scw_ref[2:3, :]
    cbufnew_ref[:, 0:D_CONV] = cbuf_ref[0, :,D_CONV:2 * D_CONV]
    cbufnew_ref[:, D_CONV:2 * D_CONV] = cu
    mix_ref[:, D_ATTN + D_SSD:D_MIX] = (gb_ref[...] * cc).astype(mix_ref.dtype)


def _dec_ssd(oattn, proj, sbuf, cbuf, st, layer, cw, cb, dtb, alog, dskip, nw, scw):
    def pblk(width, col):
        return pl.BlockSpec((DEC_BATCH, width), lambda i: (0, col // width))

    def full(a):
        return pl.BlockSpec(a.shape, lambda i: (0,) * a.ndim)

    def new(a):
        return pl.BlockSpec(a.shape[1:], lambda i: (0, 0))

    params = (cw, cb, dtb, alog, dskip, nw, scw)
    return pl.pallas_call(
        _dec_ssd_kernel,
        grid=(1,),
        in_specs=[full(oattn), pblk(D_SSD, C_Z), pblk(D_XBC, C_XBC), pblk(D_CONV, C_GB), pblk(D_CONV, C_GC),
                  pblk(D_CONV, C_U), pblk(LANES, C_DT), _layer_spec(sbuf, layer), _layer_spec(cbuf, layer),
                  _layer_spec(st, layer)] + [_layer_spec(a, layer) for a in params],
        out_specs=[pl.BlockSpec((DEC_BATCH, D_MIX), lambda i: (0, 0)), new(st), new(sbuf), new(cbuf)],
        out_shape=[jax.ShapeDtypeStruct((DEC_BATCH, D_MIX), BF16), jax.ShapeDtypeStruct(st.shape[1:], F32),
                   jax.ShapeDtypeStruct(sbuf.shape[1:], F32), jax.ShapeDtypeStruct(cbuf.shape[1:], F32)],
        scratch_shapes=[pltpu.VMEM((D_STATE_FLAT, DEC_BATCH), F32), pltpu.VMEM((D_STATE_FLAT, DEC_BATCH), F32),
                        pltpu.VMEM((D_SSD, DEC_BATCH), F32), pltpu.VMEM((LANES, DEC_BATCH), F32),
                        pltpu.VMEM((LANES, DEC_BATCH), F32), pltpu.VMEM((LANES, DEC_BATCH), F32),
                        pltpu.VMEM((D_SSD, DEC_BATCH), F32)],
        compiler_params=pltpu.CompilerParams(dimension_semantics=("arbitrary",), vmem_limit_bytes=VMEM_LIMIT),
        name="dec_ssd",
    )(oattn, proj, proj, proj, proj, proj, proj, sbuf, cbuf, st, *params)


def _swiglu(xb, wg, wu, wd):
    gate = jnp.dot(xb, wg, preferred_element_type=F32)
    up = jnp.dot(xb, wu, preferred_element_type=F32)
    return jnp.dot((_silu(gate) * up).astype(BF16), wd, preferred_element_type=F32)


def _mixer_residual_norm(mix_ref, x_ref, wo_ref, g_ref, b_ref):
    m = jnp.dot(mix_ref[...], wo_ref[0], preferred_element_type=F32)
    return _layernorm(ALPHA * x_ref[...] + m, g_ref[0], b_ref[0])


def _dense_tail_kernel(mixp_ref, mixd_ref, xp_ref, xd_ref, wo_ref, g1_ref, b1_ref, wg_ref, wu_ref, wd_ref,
                       g2_ref, b2_ref, op_ref, od_ref):
    def tail(mix_ref, x_ref, o_ref):
        x1 = _mixer_residual_norm(mix_ref, x_ref, wo_ref, g1_ref, b1_ref)
        f = _swiglu(x1.astype(BF16), wg_ref[...], wu_ref[...], wd_ref[...])
        o_ref[...] = _layernorm(ALPHA * x1 + f, g2_ref[0], b2_ref[0])

    _on_token_tile(tail, (mixp_ref, xp_ref, op_ref), (mixd_ref, xd_ref, od_ref))


def _dense_tail(mix_p, mix_d, x_p, x_d, wo, layer, g1, b1, wg, wu, wd, g2, b2):
    return pl.pallas_call(
        _dense_tail_kernel,
        grid=(N_TILES,),
        in_specs=[_p_spec(D_MIX), _d_spec(D_MIX), _p_spec(D_MODEL), _d_spec(D_MODEL), _layer_spec(wo, layer),
                  _layer_spec(g1, layer), _layer_spec(b1, layer), WHOLE_VMEM, WHOLE_VMEM, WHOLE_VMEM,
                  _layer_spec(g2, layer), _layer_spec(b2, layer)],
        out_specs=[_p_spec(D_MODEL), _d_spec(D_MODEL)],
        out_shape=[jax.ShapeDtypeStruct((N_PROMPT, D_MODEL), F32), jax.ShapeDtypeStruct((DEC_BATCH, D_MODEL), F32)],
        compiler_params=TOKEN_PARAMS, name="dense_tail",
    )(mix_p, mix_d, x_p, x_d, wo, g1, b1, wg, wu, wd, g2, b2)


TOP_K = 2
TMOE = 512
MOE_TILES = (TOP_K * N_TOK) // TMOE + N_EXPERTS
MOE_ROWS = MOE_TILES * TMOE


def _route(x, rhi_ref, rlo_ref, base, route_ref, gate_ref):
    n = x.shape[0]
    lane_i = lax.broadcasted_iota(jnp.int32, (n, LANES), 1)
    lane = lane_i.astype(F32)
    x_hi = x.astype(BF16)
    x_lo = (x - x_hi.astype(F32)).astype(BF16)
    r_hi = rhi_ref[0]
    logits = (jnp.dot(x_hi, r_hi, preferred_element_type=F32)
              + (jnp.dot(x_lo, r_hi, preferred_element_type=F32)
                 + jnp.dot(x_hi, rlo_ref[0], preferred_element_type=F32)))
    logits = jnp.where(lane_i < N_EXPERTS, logits, -jnp.inf)
    v1 = jnp.max(logits, -1, keepdims=True)
    i1 = jnp.min(jnp.where(logits == v1, lane, float(LANES)), -1, keepdims=True)
    rest = jnp.where(lane == i1, -jnp.inf, logits)
    v2 = jnp.max(rest, -1, keepdims=True)
    i2 = jnp.min(jnp.where(rest == v2, lane, float(LANES)), -1, keepdims=True)
    e2 = jnp.exp(v2 - v1)
    g1 = 1.0 / (1.0 + e2)
    oh1 = jnp.where(lane == i1, 1.0, 0.0)
    oh2 = jnp.where(lane == i2, 1.0, 0.0)
    tr = lax.broadcasted_iota(jnp.int32, (n, n), 0)
    tc = lax.broadcasted_iota(jnp.int32, (n, n), 1)
    before = jnp.where(tc < tr, 1.0, 0.0).astype(BF16)
    cum1 = jnp.dot(before, oh1.astype(BF16), preferred_element_type=F32)
    cum2 = jnp.dot(before, oh2.astype(BF16), preferred_element_type=F32)
    tot1 = jnp.sum(oh1, axis=0, keepdims=True)
    tot2 = jnp.sum(oh2, axis=0, keepdims=True)
    b = base[...]
    rank1 = jnp.sum(oh1 * (cum1 + b), -1, keepdims=True)
    rank2 = jnp.sum(oh2 * (cum2 + b + tot1), -1, keepdims=True)
    base[...] = b + tot1 + tot2
    route = jnp.where(lane_i == 0, i1, jnp.where(lane_i == 1, i2, jnp.where(lane_i == 2, rank1,
                                                                            jnp.where(lane_i == 3, rank2, 0.0))))
    route_ref[0:n, :] = route.astype(jnp.int32)
    gate_ref[0:n, :] = jnp.where(lane_i == 0, g1, jnp.where(lane_i == 1, e2 * g1, 0.0))


def _moe_head_kernel(mixp_ref, mixd_ref, xp_ref, xd_ref, wo_ref, g1_ref, b1_ref, rhi_ref, rlo_ref,
                     wg_ref, wu_ref, wd_ref,
                     x1p_ref, x1d_ref, route_ref, gate_ref, cnt_ref, wgb_ref, wub_ref, wdb_ref, base):
    @pl.when(pl.program_id(0) == 0)
    def _():
        base[...] = jnp.zeros_like(base)

    def head(mix_ref, x_ref, x1_ref):
        x1 = _mixer_residual_norm(mix_ref, x_ref, wo_ref, g1_ref, b1_ref)
        x1_ref[...] = x1
        _route(x1, rhi_ref, rlo_ref, base, route_ref, gate_ref)
        if x_ref is xp_ref:
            wgb_ref[...] = wg_ref[0].astype(BF16)
            wub_ref[...] = wu_ref[0].astype(BF16)
            wdb_ref[...] = wd_ref[0].astype(BF16)

    _on_token_tile(head, (mixp_ref, xp_ref, x1p_ref), (mixd_ref, xd_ref, x1d_ref))
    cnt_ref[...] = jnp.broadcast_to(base[...], cnt_ref.shape).astype(jnp.int32)


def _moe_head(mix_p, mix_d, x_p, x_d, wo, layer, g1, b1, r_hi, r_lo, moe_layer, wg, wu, wd):
    specs = [_cast_specs(w, moe_layer, N_PROMPT_TILES) for w in (wg, wu, wd)]
    return pl.pallas_call(
        _moe_head_kernel,
        grid=(N_TILES,),
        in_specs=[_p_spec(D_MIX), _d_spec(D_MIX), _p_spec(D_MODEL), _d_spec(D_MODEL), _layer_spec(wo, layer),
                  _layer_spec(g1, layer), _layer_spec(b1, layer), _layer_spec(r_hi, moe_layer),
                  _layer_spec(r_lo, moe_layer)]
                 + [s[0] for s in specs],
        out_specs=[_p_spec(D_MODEL), _d_spec(D_MODEL), _tile_spec(LANES), _tile_spec(LANES),
                   pl.BlockSpec((8, LANES), lambda i: (0, 0))] + [s[1] for s in specs],
        out_shape=[jax.ShapeDtypeStruct((N_PROMPT, D_MODEL), F32), jax.ShapeDtypeStruct((DEC_BATCH, D_MODEL), F32),
                   jax.ShapeDtypeStruct((N_TOK, LANES), jnp.int32), jax.ShapeDtypeStruct((N_TOK, LANES), F32),
                   jax.ShapeDtypeStruct((8, LANES), jnp.int32)]
                  + [jax.ShapeDtypeStruct(w.shape[1:], BF16) for w in (wg, wu, wd)],
        scratch_shapes=[pltpu.VMEM((1, LANES), F32)],
        compiler_params=TOKEN_PARAMS, name="moe_head",
    )(mix_p, mix_d, x_p, x_d, wo, g1, b1, r_hi, r_lo, wg, wu, wd)


def _row_copy(src, src_row, dst, dst_row, sem):
    return pltpu.make_async_copy(src.at[pl.ds(src_row, 1)], dst.at[pl.ds(dst_row, 1)], sem)


DMA_UNROLL = 8


def _dispatch_kernel(pos1_ref, pos2_ref, cnt_ref, off_ref, pad_ref, xp_ref, xd_ref, xs_hbm, sem):
    start = pl.program_id(0) * TM

    def scatter(x_ref):
        n = x_ref.shape[0]

        def issue(r, carry):
            t = start + r
            _row_copy(x_ref, r, xs_hbm, pos1_ref[t], sem).start(priority=0)
            _row_copy(x_ref, r, xs_hbm, pos2_ref[t], sem).start(priority=1)
            return carry

        lax.fori_loop(0, n, issue, 0, unroll=DMA_UNROLL)
        for _ in range(TOP_K):
            pltpu.make_async_copy(x_ref, xs_hbm.at[pl.ds(0, n)], sem).wait()

    def scatter_and_pad(x_ref):
        scatter(x_ref)

        def drain(q, carry):
            _row_copy(x_ref, 0, xs_hbm, 0, sem).wait()
            return carry

        for e in range(N_EXPERTS):
            lo = off_ref[e] + cnt_ref[e]

            def fill(q, carry, lo=lo):
                _row_copy(x_ref, 0, xs_hbm, lo + q, sem).start()
                return carry

            lax.fori_loop(0, pad_ref[e], fill, 0)
            lax.fori_loop(0, pad_ref[e], drain, 0)

    _on_token_tile(lambda x_ref, pad: scatter_and_pad(x_ref) if pad else scatter(x_ref),
                   (xp_ref, False), (xd_ref, True))

def _dispatch(x_p, x_d, pos1, pos2, counts, offs, pads):
    return pl.pallas_call(
        _dispatch_kernel,
        grid_spec=pltpu.PrefetchScalarGridSpec(
            num_scalar_prefetch=5, grid=(N_TILES,),
            in_specs=[_p_spec(D_MODEL), _d_spec(D_MODEL)],
            out_specs=pl.BlockSpec(memory_space=pl.ANY),
            scratch_shapes=[pltpu.SemaphoreType.DMA(())]),
        out_shape=jax.ShapeDtypeStruct((MOE_ROWS, D_MODEL), F32),
        compiler_params=TOKEN_PARAMS, name="moe_dispatch",
    )(pos1, pos2, counts, offs, pads, x_p, x_d)


def _gffn_kernel(te_ref, nv_ref, xs_ref, wg_ref, wu_ref, wd_ref, ys_ref):
    @pl.when(pl.program_id(0) < nv_ref[0])
    def _():
        ys_ref[...] = _swiglu(xs_ref[...].astype(BF16), wg_ref[0], wu_ref[0], wd_ref[0])


def _gffn(xs, tile_expert, n_valid, wg, wu, wd):
    def tile(i, te, nv):
        return (jnp.minimum(i, nv[0] - 1), 0)

    def expert(i, te, nv):
        return (te[jnp.minimum(i, nv[0] - 1)], 0, 0)

    return pl.pallas_call(
        _gffn_kernel,
        grid_spec=pltpu.PrefetchScalarGridSpec(
            num_scalar_prefetch=2, grid=(MOE_TILES,),
            in_specs=[pl.BlockSpec((TMOE, D_MODEL), tile), pl.BlockSpec((1, D_MODEL, D_FF), expert),
                      pl.BlockSpec((1, D_MODEL, D_FF), expert), pl.BlockSpec((1, D_FF, D_MODEL), expert)],
            out_specs=pl.BlockSpec((TMOE, D_MODEL), tile)),
        out_shape=jax.ShapeDtypeStruct((MOE_ROWS, D_MODEL), F32),
        compiler_params=pltpu.CompilerParams(dimension_semantics=("arbitrary",), vmem_limit_bytes=VMEM_LIMIT),
        name="moe_ffn",
    )(tile_expert, n_valid, xs, wg, wu, wd)


def _combine_kernel(pos1_ref, pos2_ref, xp_ref, xd_ref, gate_ref, ys_hbm, g_ref, b_ref, op_ref, od_ref, rows, sem):
    start = pl.program_id(0) * TM

    def combine(x_ref, o_ref):
        n = x_ref.shape[0]

        def issue(r, carry):
            t = start + r
            pltpu.make_async_copy(ys_hbm.at[pl.ds(pos1_ref[t], 1)], rows.at[0, pl.ds(r, 1)], sem).start(priority=0)
            pltpu.make_async_copy(ys_hbm.at[pl.ds(pos2_ref[t], 1)], rows.at[1, pl.ds(r, 1)], sem).start(priority=1)
            return carry

        lax.fori_loop(0, n, issue, 0, unroll=DMA_UNROLL)
        for s in range(TOP_K):
            pltpu.make_async_copy(ys_hbm.at[pl.ds(0, n)], rows.at[s, pl.ds(0, n)], sem).wait()
        gate = gate_ref[0:n, :]
        f = gate[:, 0:1] * rows[0, 0:n, :] + gate[:, 1:2] * rows[1, 0:n, :]
        o_ref[...] = _layernorm(ALPHA * x_ref[...] + f, g_ref[0], b_ref[0])

    _on_token_tile(combine, (xp_ref, op_ref), (xd_ref, od_ref))


def _combine_ln(x_p, x_d, gates, ys, pos1, pos2, layer, g, b):
    return pl.pallas_call(
        _combine_kernel,
        grid_spec=pltpu.PrefetchScalarGridSpec(
            num_scalar_prefetch=2, grid=(N_TILES,),
            in_specs=[_p_spec(D_MODEL), _d_spec(D_MODEL), _tile_spec(LANES), pl.BlockSpec(memory_space=pl.ANY),
                      _layer_spec(g, layer), _layer_spec(b, layer)],
            out_specs=[_p_spec(D_MODEL), _d_spec(D_MODEL)],
            scratch_shapes=[pltpu.VMEM((TOP_K, TM, D_MODEL), F32), pltpu.SemaphoreType.DMA(())]),
        out_shape=[jax.ShapeDtypeStruct((N_PROMPT, D_MODEL), F32), jax.ShapeDtypeStruct((DEC_BATCH, D_MODEL), F32)],
        compiler_params=TOKEN_PARAMS, name="moe_combine_ln",
    )(pos1, pos2, x_p, x_d, gates, ys, g, b)


def _moe_tail(mix_p, mix_d, x_p, x_d, wo, layer, g1, b1, r_hi, r_lo, moe_layer, wg, wu, wd, g2, b2):
    flat = [w.reshape(w.shape[0], -1, w.shape[-1]) for w in (wg, wu, wd)]
    x1_p, x1_d, route, gates, cnt, wg_b, wu_b, wd_b = _moe_head(mix_p, mix_d, x_p, x_d, wo, layer, g1, b1,
                                                                r_hi, r_lo, moe_layer, *flat)
    wg_b, wu_b, wd_b = (o.reshape(w.shape[1:]) for o, w in zip((wg_b, wu_b, wd_b), (wg, wu, wd)))
    counts = cnt[0, :N_EXPERTS]
    padded = (counts + TMOE - 1) // TMOE * TMOE
    ends = jnp.cumsum(padded)
    offs = ends - padded
    pos1 = jnp.take(offs, route[:, 0]) + route[:, 2]
    pos2 = jnp.take(offs, route[:, 1]) + route[:, 3]
    tile_ends = ends // TMOE
    tile_ids = jnp.arange(MOE_TILES, dtype=jnp.int32)
    tile_expert = jnp.minimum(jnp.sum(tile_ids[:, None] >= tile_ends[None, :], axis=1), N_EXPERTS - 1)
    xs = _dispatch(x1_p, x1_d, pos1, pos2, counts, offs, padded - counts)
    ys = _gffn(xs, tile_expert.astype(jnp.int32), tile_ends[N_EXPERTS - 1:], wg_b, wu_b, wd_b)
    return _combine_ln(x1_p, x1_d, gates, ys, pos1, pos2, layer, g2, b2)


def _rope_tables(pos):
    half = HEAD_DIM // 2
    inv_freq = ROPE_THETA ** (-jnp.arange(half, dtype=F32) / half)
    ang = pos.astype(F32)[:, None] * inv_freq[None, :]
    cos = jnp.cos(ang)
    sin = jnp.sin(ang)
    return jnp.concatenate([cos] * 4, axis=1), jnp.concatenate([-sin, sin, -sin, sin], axis=1)


def _pad_lanes(a):
    return jnp.pad(a, [(0, 0)] * (a.ndim - 1) + [(0, LANES - a.shape[-1])])


def kernel(x_prompt, x_sample, cache_win_k, cache_win_v, state_ssd, state_ssd_conv, state_conv, w_in, attn_sinks,
           ssd_conv_w, ssd_conv_b, ssd_dt_bias, ssd_a_log, ssd_d, ssd_norm_w, sconv_w, w_out, ln1_g, ln1_b, ln2_g,
           ln2_b, ffn_w_gate, ffn_w_up, ffn_w_down, moe_router, moe_w_gate, moe_w_up, moe_w_down):
    x_p = x_prompt.reshape(N_PROMPT, D_MODEL)
    x_d = x_sample.reshape(DEC_BATCH, D_MODEL)
    cos_p, sin_p = _rope_tables(jnp.arange(SEQ, dtype=jnp.int32))
    cos_s, sin_s = _rope_tables(PAST_LEN + jnp.arange(1, dtype=jnp.int32))
    split = C_GB + SSD_HEADS
    w_in_r = jnp.concatenate([w_in[:, :, :C_GB], w_in[:, :, split:], _pad_lanes(w_in[:, :, C_GB:split])],
                             axis=-1).astype(BF16)
    w_out_b = w_out.astype(BF16)
    dtb = _pad_lanes(ssd_dt_bias)[:, None, :]
    alog = _pad_lanes(ssd_a_log)[:, None, :]
    dskip = jnp.repeat(ssd_d, SSD_HEAD_DIM, axis=-1)[:, None, :]
    router = _pad_lanes(moe_router)
    r_hi = router.astype(BF16)
    r_lo = (router - r_hi.astype(F32)).astype(BF16)
    sink_rows = jnp.broadcast_to(attn_sinks[:, :, None], (DEPTH, N_HEADS, LANES))
    kc = cache_win_k.reshape(DEPTH, DEC_BATCH, WINDOW, D_KV)
    vc = cache_win_v.reshape(DEPTH, DEC_BATCH, WINDOW, D_KV)
    st = state_ssd.reshape(DEPTH, DEC_BATCH, SSD_HEADS * D_STATE_FLAT)
    sbuf = state_ssd_conv.reshape(DEPTH, DEC_BATCH, (SSD_CONV - 1) * D_XBC)
    cbuf = state_conv.reshape(DEPTH, DEC_BATCH, (CONV_WIDTH - 1) * D_CONV)

    params = (ssd_conv_w, ssd_conv_b[:, None, :], dtb, alog, dskip, ssd_norm_w[:, None, :], sconv_w)
    ln1 = (ln1_g[:, None, :], ln1_b[:, None, :])
    ln2 = (ln2_g[:, None, :], ln2_b[:, None, :])

    outs = {n: [] for n in ("pk", "pv", "ph", "psc", "pc", "sk", "sv", "sh", "ssc", "sc")}
    for i in range(DEPTH):
        j = i // 2
        dense = i % 2 == 0
        if dense:
            proj_p, proj_d, wg_b, wu_b, wd_b = _inproj(x_p, x_d, w_in_r, i, (ffn_w_gate, ffn_w_up, ffn_w_down), j)
        else:
            proj_p, proj_d = _inproj(x_p, x_d, w_in_r, i)
        proj_p = proj_p.reshape(BATCH, SEQ, D_PROJ)
        mix_p, klast, hlast, culast = _mixer_prompt(proj_p, attn_sinks[i], cos_p, sin_p, i, *params)
        oattn, knew = _dec_attn(proj_d, kc, vc, i, cos_s, sin_s, sink_rows)
        mix_d, stnew, sbufnew, cbufnew = _dec_ssd(oattn, proj_d, sbuf, cbuf, st, i, *params)
        mix_p = mix_p.reshape(N_PROMPT, D_MIX)
        if dense:
            x_p, x_d = _dense_tail(mix_p, mix_d, x_p, x_d, w_out_b, i, *ln1, wg_b, wu_b, wd_b, *ln2)
        else:
            x_p, x_d = _moe_tail(mix_p, mix_d, x_p, x_d, w_out_b, i, *ln1, r_hi, r_lo, j,
                                 moe_w_gate, moe_w_up, moe_w_down, *ln2)

        def tail(n_rows, col, width):
            return proj_p[:, SEQ - n_rows:, col:col + width]

        outs["pk"].append(klast.reshape(BATCH, WINDOW, N_KV_HEADS, HEAD_DIM))
        outs["pv"].append(tail(WINDOW, C_V, D_KV).reshape(BATCH, WINDOW, N_KV_HEADS, HEAD_DIM))
        outs["ph"].append(hlast)
        outs["psc"].append(tail(SSD_CONV - 1, C_XBC, D_XBC))
        outs["pc"].append(culast[:, 8 - (CONV_WIDTH - 1):, :])
        outs["sk"].append(knew)
        outs["sv"].append(proj_d[:, C_V:C_V + D_KV])
        outs["sh"].append(stnew.reshape(DEC_BATCH, SSD_HEADS, SSD_HEAD_DIM, SSD_STATE))
        outs["ssc"].append(sbufnew.reshape(DEC_BATCH, SSD_CONV - 1, D_XBC))
        outs["sc"].append(cbufnew.reshape(DEC_BATCH, CONV_WIDTH - 1, D_CONV))
    y_prompt = x_p.reshape(BATCH, SEQ, D_MODEL)
    y_sample = x_d.reshape(DEC_BATCH, 1, D_MODEL)
    stk = lambda n: jnp.stack(outs[n])

    def slide(cache, new):
        win = jnp.concatenate([cache[:, :, 1:], stk(new)[:, :, None, :]], axis=2)
        return win.reshape(DEPTH, DEC_BATCH, WINDOW, N_KV_HEADS, HEAD_DIM)

    return (y_prompt, y_sample, stk("pk"), stk("pv"), stk("ph"), stk("psc"), stk("pc"),
            slide(kc, "sk"), slide(vc, "sv"), stk("sh"), stk("ssc"), stk("sc"))
```
